```python
import jax, jax.numpy as jnp
from jax import lax
import numpy as np

D_MODEL = 1024
BATCH = 2
SEQ = 8192
DEPTH = 2
DEC_BATCH = 128
DEC_SEQ = 4
PAST_LEN = 2048
PAGE_SIZE = 128

HEAD_DIM = 64
LRU_WIDTH = D_MODEL // 2
LRU_BLOCKS = LRU_WIDTH // HEAD_DIM
LRU_BW = LRU_WIDTH // LRU_BLOCKS
CONV_W = 4
LRU_C = 8.0
NSA_HEADS = (D_MODEL // 2) // HEAD_DIM
NSA_KV = 2
NSA_GROUP = NSA_HEADS // NSA_KV
CMP_LEN = 32
SEL_LEN = 64
CMP_PER_SEL = SEL_LEN // CMP_LEN
N_SEL = 16
SW_WIN = 512
FORCE = 1000.0
C_HEADS = D_MODEL // HEAD_DIM
DIL_PATTERNS = ((128, 1), (512, 4), (2048, 16))
C_WINDOW = 2048
N_GROUPS = 4
EXP_PER_GROUP = 8
N_EXPERTS = N_GROUPS * EXP_PER_GROUP
D_EXPERT = D_MODEL // 2
TOP_K = 2
MOE_BLK = 128
Q_BLOCK = 128
ROPE_THETA = 10000.0
EPS = 1e-6
NEG = -1e30
TINY = 1e-30
N_EVEN = (DEPTH + 1) // 2
N_ODD = DEPTH // 2
EVEN_SIZES = (LRU_WIDTH, LRU_WIDTH, NSA_HEADS * HEAD_DIM) + (NSA_KV * HEAD_DIM,) * 6 + (3 * NSA_HEADS,)
EVEN_COLS = sum(EVEN_SIZES)
EVEN_OUT = LRU_WIDTH + NSA_HEADS * HEAD_DIM
C_WIDTH = C_HEADS * HEAD_DIM

kernel_name = 'hybrid_rglru_nsa_dilated_hmoe_step'


def rms_norm(x, g):
    xf = x.astype(jnp.float32)
    y = xf * lax.rsqrt(jnp.mean(xf * xf, axis=-1, keepdims=True) + EPS)
    return (y * g.astype(jnp.float32)).astype(x.dtype)


def rope(x, pos):
    half = HEAD_DIM // 2
    freq = ROPE_THETA ** (-jnp.arange(half, dtype=jnp.float32) / half)
    ang = pos.astype(jnp.float32)[:, None] * freq[None, :]
    cos = jnp.cos(ang)[None, :, None, :]
    sin = jnp.sin(ang)[None, :, None, :]
    xf = x.astype(jnp.float32)
    x1, x2 = xf[..., :half], xf[..., half:]
    return jnp.concatenate([x1 * cos - x2 * sin, x2 * cos + x1 * sin], axis=-1).astype(x.dtype)


def masked_softmax(s, mask):
    s = jnp.where(mask, s.astype(jnp.float32), NEG)
    m = jnp.max(s, axis=-1, keepdims=True)
    e = jnp.where(mask, jnp.exp(s - m), 0.0)
    den = jnp.maximum(jnp.sum(e, axis=-1, keepdims=True), TINY)
    return e / den, m + jnp.log(den)


def split_cols(z, sizes):
    out, o = [], 0
    for s in sizes:
        out.append(z[..., o:o + s])
        o += s
    return out


def over_query_blocks(fn, T):
    if T > Q_BLOCK and T % Q_BLOCK == 0:
        out = lax.map(lambda i: fn(i * Q_BLOCK, Q_BLOCK), jnp.arange(T // Q_BLOCK))
        return jnp.swapaxes(out, 0, 1).reshape(out.shape[1], T, out.shape[-1])
    return fn(0, T)


def gather_pages(pool, page_table):
    g = pool[page_table]
    return g.reshape(g.shape[0], g.shape[1] * g.shape[2], g.shape[3], g.shape[4])


def causal_conv(u, buf, w, b):
    T = u.shape[1]
    up = jnp.concatenate([buf, u], axis=1)
    y = b + up[:, 0:T] * w[0]
    for j in range(1, CONV_W):
        y = y + up[:, j:j + T] * w[j]
    return y, up[:, T:]


def _lin_comb(l, r):
    return (l[0] * r[0], r[0] * l[1] + r[1])


def rg_lru(u, h0, wa, ba, wi, bi, lam):
    B, T, W = u.shape
    ub = u.reshape(B, T, LRU_BLOCKS, LRU_BW)
    r = jax.nn.sigmoid((jnp.einsum('btki,kij->btkj', ub, wa).reshape(B, T, W) + ba).astype(jnp.float32))
    gi = jax.nn.sigmoid((jnp.einsum('btki,kij->btkj', ub, wi).reshape(B, T, W) + bi).astype(jnp.float32))
    log_a = -LRU_C * r * jax.nn.softplus(-lam.astype(jnp.float32))
    a = jnp.exp(log_a)
    b = jnp.sqrt(-jnp.expm1(2.0 * log_a)) * (gi * u.astype(jnp.float32))
    b = b.at[:, 0].add(a[:, 0] * h0.astype(jnp.float32))
    _, h = lax.associative_scan(_lin_comb, (a, b), axis=1)
    return h.astype(u.dtype), h[:, -1].astype(u.dtype)


def compress(rows, w, t_pad):
    B, T = rows.shape[:2]
    r = jnp.pad(rows, ((0, 0), (0, t_pad - T), (0, 0), (0, 0)))
    r = r.reshape(B, t_pad // CMP_LEN, CMP_LEN, NSA_KV, HEAD_DIM)
    return jnp.einsum('bnjgd,jde->bnge', r, w)


def sel_blocks(rows, t_pad):
    B, T = rows.shape[:2]
    r = jnp.pad(rows, ((0, 0), (0, t_pad - T), (0, 0), (0, 0)))
    return r.reshape(B, t_pad // SEL_LEN, SEL_LEN, NSA_KV, HEAD_DIM).transpose(0, 3, 1, 2, 4)


def nsa_attend(q, q_rot, gates, qpos, wstart, kc, vc, ksg, vsg, kw_p, vw_p):
    B, Q = q.shape[:2]
    G, M = NSA_KV, NSA_GROUP
    scale = HEAD_DIM ** -0.5
    qg = q.reshape(B, Q, G, M, HEAD_DIM)
    qr = q_rot.reshape(B, Q, G, M, HEAD_DIM)
    ncb = kc.shape[1]
    s_c = jnp.einsum('bqgmd,bngd->bgmqn', qg, kc) * scale
    cmask = ((jnp.arange(ncb) + 1) * CMP_LEN - 1)[None, :] <= qpos[:, None]
    p_c, _ = masked_softmax(s_c, cmask)
    o_c = jnp.einsum('bgmqn,bngd->bqgmd', p_c.astype(vc.dtype), vc)
    nsb = ksg.shape[2]
    imp = p_c.sum(axis=2).reshape(B, G, Q, nsb, CMP_PER_SEL).sum(-1)
    sb = jnp.arange(nsb)[None, :]
    cur = (qpos // SEL_LEN)[:, None]
    valid = sb * SEL_LEN <= qpos[:, None]
    forced = (sb == 0) | (sb == cur) | (sb == cur - 1)
    score = jnp.where(valid, imp + jnp.where(forced, FORCE, 0.0), NEG)
    top_s, top_i = lax.top_k(score, min(N_SEL, nsb))
    n = top_i.shape[-1]
    bi = jnp.arange(B)[:, None, None, None]
    gi = jnp.arange(G)[None, :, None, None]
    k_sel = ksg[bi, gi, top_i]
    v_sel = vsg[bi, gi, top_i]
    kpos = top_i[..., None] * SEL_LEN + jnp.arange(SEL_LEN)
    smask = (top_s > 0.5 * NEG)[..., None] & (kpos <= qpos[None, None, :, None, None])
    s_s = jnp.einsum('bqgmd,bgqnld->bgmqnl', qr, k_sel).reshape(B, G, M, Q, n * SEL_LEN) * scale
    p_s, _ = masked_softmax(s_s, smask.reshape(B, G, 1, Q, n * SEL_LEN))
    o_s = jnp.einsum('bgmqk,bgqkd->bqgmd', p_s.astype(v_sel.dtype), v_sel.reshape(B, G, Q, n * SEL_LEN, HEAD_DIM))
    span = SW_WIN + Q
    kwb = lax.dynamic_slice_in_dim(kw_p, wstart, span, axis=1)
    vwb = lax.dynamic_slice_in_dim(vw_p, wstart, span, axis=1)
    qloc = wstart + jnp.arange(Q)
    kloc = wstart - SW_WIN + jnp.arange(span)
    dist = qloc[:, None] - kloc[None, :]
    wmask = (dist >= 0) & (dist <= SW_WIN) & (kloc[None, :] >= 0)
    s_w = jnp.einsum('bqgmd,bkgd->bgmqk', qr, kwb) * scale
    p_w, _ = masked_softmax(s_w, wmask)
    o_w = jnp.einsum('bgmqk,bkgd->bqgmd', p_w.astype(vwb.dtype), vwb)
    gt = gates.reshape(B, Q, G, M, 3).astype(o_c.dtype)
    o = gt[..., 0:1] * o_c + gt[..., 1:2] * o_s + gt[..., 2:3] * o_w
    return o.reshape(B, Q, NSA_HEADS * HEAD_DIM)


def even_mixer(h, p0, past, P, e):
    lru_h0, conv_buf, past_ck, past_cv, past_sk, past_sv, win_k, win_v = past
    B, T, _ = h.shape
    z = h @ P['w_in_even'][e]
    u, y, q, ck, cv, sk, sv, wk, wv, g = split_cols(z, EVEN_SIZES)
    uc, conv_new = causal_conv(u, conv_buf, P['conv_w'][e], P['conv_b'][e])
    hl, h_last = rg_lru(uc, lru_h0, P['lru_wa'][e], P['lru_ba'][e], P['lru_wi'][e], P['lru_bi'][e], P['lru_lambda'][e])
    lru_out = hl * jax.nn.gelu(y)
    pos = p0 + jnp.arange(T)
    kg = P['nsa_k_gain'][e]
    kvs = (B, T, NSA_KV, HEAD_DIM)
    q = rms_norm(q.reshape(B, T, NSA_HEADS, HEAD_DIM), P['nsa_q_gain'][e])
    q_rot = rope(q, pos)
    ck = rms_norm(ck.reshape(kvs), kg[0])
    cv = cv.reshape(kvs)
    sk = rope(rms_norm(sk.reshape(kvs), kg[1]), pos)
    sv = sv.reshape(kvs)
    wk = rope(rms_norm(wk.reshape(kvs), kg[2]), pos)
    wv = wv.reshape(kvs)
    gates = jax.nn.sigmoid(g.astype(jnp.float32)).reshape(B, T, NSA_HEADS, 3)
    ck_all = jnp.concatenate([past_ck, ck], axis=1)
    cv_all = jnp.concatenate([past_cv, cv], axis=1)
    sk_all = jnp.concatenate([past_sk, sk], axis=1)
    sv_all = jnp.concatenate([past_sv, sv], axis=1)
    t_k = ck_all.shape[1]
    t_pad = -(-t_k // SEL_LEN) * SEL_LEN
    kc = compress(ck_all, P['w_cmp_k'][e], t_pad)
    vc = compress(cv_all, P['w_cmp_v'][e], t_pad)
    ksg = sel_blocks(sk_all, t_pad)
    vsg = sel_blocks(sv_all, t_pad)
    wk_all = jnp.concatenate([win_k, wk], axis=1)
    wv_all = jnp.concatenate([win_v, wv], axis=1)
    wb = win_k.shape[1]
    pad_w = ((0, 0), (SW_WIN, 0), (0, 0), (0, 0))
    wk_p = jnp.pad(wk_all, pad_w)
    wv_p = jnp.pad(wv_all, pad_w)

    def block(q0, qb):
        sl = lambda a: lax.dynamic_slice_in_dim(a, q0, qb, axis=1)
        qpos = p0 + q0 + jnp.arange(qb)
        return nsa_attend(sl(q), sl(q_rot), sl(gates), qpos, wb + q0, kc, vc, ksg, vsg, wk_p, wv_p)

    nsa_out = over_query_blocks(block, T)
    mix = jnp.concatenate([lru_out, nsa_out.astype(lru_out.dtype)], axis=-1) @ P['w_out_even'][e]
    keep = min(SW_WIN, wk_all.shape[1])
    return mix, (h_last, conv_new, ck, cv, sk, sv, wk_all[:, -keep:], wv_all[:, -keep:])


def dilated_attend(q, kp, vp, qloc):
    scale = HEAD_DIM ** -0.5
    outs, lses = [], []
    for w, d in DIL_PATTERNS:
        kl = qloc[:, None] - d * jnp.arange(w // d + 1)[None, :]
        kg = kp[:, kl + C_WINDOW]
        vg = vp[:, kl + C_WINDOW]
        s = jnp.einsum('bqhd,bqnhd->bhqn', q, kg) * scale
        p, lse = masked_softmax(s, kl >= 0)
        outs.append(jnp.einsum('bhqn,bqnhd->bqhd', p.astype(vg.dtype), vg))
        lses.append(lse)
    wts = jax.nn.softmax(jnp.stack(lses), axis=0)
    o = jnp.swapaxes(wts[0], 1, 2).astype(outs[0].dtype) * outs[0]
    for i in range(1, len(DIL_PATTERNS)):
        o = o + jnp.swapaxes(wts[i], 1, 2).astype(outs[i].dtype) * outs[i]
    return o.reshape(o.shape[0], o.shape[1], C_WIDTH)


def odd_mixer(h, p0, past, P, o):
    buf_k, buf_v = past
    B, T, _ = h.shape
    z = h @ P['w_in_odd'][o]
    q, k, v = split_cols(z, (C_WIDTH,) * 3)
    shp = (B, T, C_HEADS, HEAD_DIM)
    pos = p0 + jnp.arange(T)
    q = rope(rms_norm(q.reshape(shp), P['dil_q_gain'][o]), pos)
    k = rope(rms_norm(k.reshape(shp), P['dil_k_gain'][o]), pos)
    v = v.reshape(shp)
    k_all = jnp.concatenate([buf_k, k], axis=1)
    v_all = jnp.concatenate([buf_v, v], axis=1)
    wb = buf_k.shape[1]
    pad = ((0, 0), (C_WINDOW, 0), (0, 0), (0, 0))
    kp = jnp.pad(k_all, pad)
    vp = jnp.pad(v_all, pad)

    def block(q0, qb):
        return dilated_attend(lax.dynamic_slice_in_dim(q, q0, qb, axis=1), kp, vp, wb + q0 + jnp.arange(qb))

    att = over_query_blocks(block, T)
    keep = min(C_WINDOW, k_all.shape[1])
    return att @ P['w_out_odd'][o], (k_all[:, -keep:], v_all[:, -keep:])


def routed_experts(xf, eid, wts, w_gate, w_up, w_down):
    N, D = xf.shape
    M = N * TOP_K
    e_flat = eid.reshape(M)
    tok = jnp.repeat(jnp.arange(N, dtype=jnp.int32), TOP_K)
    w_flat = wts.reshape(M)
    order = jnp.argsort(e_flat)
    e_sorted = e_flat[order]
    counts = jnp.bincount(e_flat, length=N_EXPERTS)
    padded = (counts + MOE_BLK - 1) // MOE_BLK * MOE_BLK
    start = jnp.cumsum(counts) - counts
    pend = jnp.cumsum(padded)
    pstart = pend - padded
    dest = pstart[e_sorted] + (jnp.arange(M) - start[e_sorted])
    n_blocks = -(-(M + N_EXPERTS * (MOE_BLK - 1)) // MOE_BLK)
    n_slots = n_blocks * MOE_BLK
    slot_tok = jnp.full((n_slots,), N, jnp.int32).at[dest].set(tok[order])
    slot_w = jnp.zeros((n_slots,), xf.dtype).at[dest].set(w_flat[order].astype(xf.dtype))
    blk_e = jnp.minimum(jnp.searchsorted(pend, jnp.arange(n_blocks) * MOE_BLK, side='right'), N_EXPERTS - 1)
    x_pad = jnp.concatenate([xf, jnp.zeros((1, D), xf.dtype)], axis=0)

    def run(args):
        toks, w, e = args
        xb = x_pad[toks]
        hid = jax.nn.silu(xb @ w_gate[e]) * (xb @ w_up[e])
        return (hid @ w_down[e]) * w[:, None]

    yb = lax.map(run, (slot_tok.reshape(n_blocks, MOE_BLK), slot_w.reshape(n_blocks, MOE_BLK), blk_e))
    y = jnp.zeros((N + 1, D), xf.dtype).at[slot_tok].add(yb.reshape(n_slots, D))
    return y[:N]


def hier_moe(x, P, layer):
    B, T, D = x.shape
    N = B * T
    xf = x.reshape(N, D)
    rows = jnp.arange(N)
    g_logit = (xf @ P['w_router_group'][layer] + P['b_router_group'][layer]).astype(jnp.float32)
    g_top = jnp.argmax(g_logit, axis=-1)
    g_w = jax.nn.softmax(g_logit, axis=-1)[rows, g_top]
    e_logit = (xf @ P['w_router_exp'][layer] + P['b_router_exp'][layer]).astype(jnp.float32)
    e_logit = e_logit.reshape(N, N_GROUPS, EXP_PER_GROUP)[rows, g_top]
    top_l, top_i = lax.top_k(e_logit, TOP_K)
    top_p = jax.nn.softmax(top_l, axis=-1)
    eid = (g_top[:, None] * EXP_PER_GROUP + top_i).astype(jnp.int32)
    wts = g_w[:, None] * top_p
    y = routed_experts(xf, eid, wts, P['w_exp_gate'][layer], P['w_exp_up'][layer], P['w_exp_down'][layer])
    return y.reshape(B, T, D)


def trunk(x, p0, even_past, odd_past, P):
    even_new, odd_new = [], []
    for layer in range(DEPTH):
        hn = rms_norm(x, P['norm_mix'][layer])
        if layer % 2 == 0:
            mix, st = even_mixer(hn, p0, even_past[layer // 2], P, layer // 2)
            even_new.append(st)
        else:
            mix, st = odd_mixer(hn, p0, odd_past[layer // 2], P, layer // 2)
            odd_new.append(st)
        x = x + mix.astype(x.dtype)
        x = x + hier_moe(rms_norm(x, P['norm_ffn'][layer]), P, layer).astype(x.dtype)
    return x, even_new, odd_new


def stack_layers(states):
    return [jnp.stack(items) for items in zip(*states)]


def setup_inputs(seed: int = 0) -> dict:
    key = jax.random.key(seed)
    ks = jax.random.split(key, 48)

    def nrm(i, shape, scale):
        return jax.random.normal(ks[i], shape, jnp.float32) * scale

    n_pages = PAST_LEN // PAGE_SIZE
    n_pool = (5 * DEC_BATCH * n_pages) // 4
    paged = (N_EVEN, n_pool, PAGE_SIZE, NSA_KV, HEAD_DIM)
    win = (N_EVEN, DEC_BATCH, min(SW_WIN, PAST_LEN), NSA_KV, HEAD_DIM)
    dil = (N_ODD, DEC_BATCH, min(C_WINDOW, PAST_LEN), C_HEADS, HEAD_DIM)
    page_table = jax.random.permutation(ks[0], n_pool)[:DEC_BATCH * n_pages].reshape(DEC_BATCH, n_pages).astype(jnp.int32)
    a0 = jax.random.uniform(ks[1], (N_EVEN, LRU_WIDTH), jnp.float32, 0.9, 0.999)
    blk = (N_EVEN, LRU_BLOCKS, LRU_BW, LRU_BW)
    cmp = (N_EVEN, CMP_LEN, HEAD_DIM, HEAD_DIM)
    return {
        'x_prompt': nrm(2, (BATCH, SEQ, D_MODEL), 1.0),
        'x_sample': nrm(3, (DEC_BATCH, DEC_SEQ, D_MODEL), 1.0),
        'state_lru_h': nrm(4, (N_EVEN, DEC_BATCH, LRU_WIDTH), 0.5),
        'state_lru_conv': nrm(5, (N_EVEN, DEC_BATCH, CONV_W - 1, LRU_WIDTH), 1.0),
        'cache_cmp_k': nrm(6, paged, 1.0),
        'cache_cmp_v': nrm(7, paged, 1.0),
        'cache_sel_k': nrm(8, paged, 1.0),
        'cache_sel_v': nrm(9, paged, 1.0),
        'cache_win_k': nrm(10, win, 1.0),
        'cache_win_v': nrm(11, win, 1.0),
        'cache_dil_k': nrm(12, dil, 1.0),
        'cache_dil_v': nrm(13, dil, 1.0),
        'page_table': page_table,
        'norm_mix': 1.0 + nrm(14, (DEPTH, D_MODEL), 0.1),
        'norm_ffn': 1.0 + nrm(15, (DEPTH, D_MODEL), 0.1),
        'w_in_even': nrm(16, (N_EVEN, D_MODEL, EVEN_COLS), D_MODEL ** -0.5),
        'conv_w': nrm(17, (N_EVEN, CONV_W, LRU_WIDTH), CONV_W ** -0.5),
        'conv_b': nrm(18, (N_EVEN, LRU_WIDTH), 0.01),
        'lru_wa': nrm(19, blk, LRU_BW ** -0.5),
        'lru_ba': nrm(20, (N_EVEN, LRU_WIDTH), 0.01),
        'lru_wi': nrm(21, blk, LRU_BW ** -0.5),
        'lru_bi': nrm(22, (N_EVEN, LRU_WIDTH), 0.01),
        'lru_lambda': jnp.log(a0) - jnp.log1p(-a0),
        'nsa_q_gain': 1.0 + nrm(23, (N_EVEN, HEAD_DIM), 0.1),
        'nsa_k_gain': 1.0 + nrm(24, (N_EVEN, 3, HEAD_DIM), 0.1),
        'w_cmp_k': nrm(25, cmp, (CMP_LEN * HEAD_DIM) ** -0.5),
        'w_cmp_v': nrm(26, cmp, (CMP_LEN * HEAD_DIM) ** -0.5),
        'w_out_even': nrm(27, (N_EVEN, EVEN_OUT, D_MODEL), EVEN_OUT ** -0.5),
        'w_in_odd': nrm(28, (N_ODD, D_MODEL, 3 * C_WIDTH), D_MODEL ** -0.5),
        'dil_q_gain': 1.0 + nrm(29, (N_ODD, HEAD_DIM), 0.1),
        'dil_k_gain': 1.0 + nrm(30, (N_ODD, HEAD_DIM), 0.1),
        'w_out_odd': nrm(31, (N_ODD, C_WIDTH, D_MODEL), C_WIDTH ** -0.5),
        'w_router_group': nrm(32, (DEPTH, D_MODEL, N_GROUPS), D_MODEL ** -0.5),
        'b_router_group': nrm(33, (DEPTH, N_GROUPS), 0.01),
        'w_router_exp': nrm(34, (DEPTH, D_MODEL, N_EXPERTS), D_MODEL ** -0.5),
        'b_router_exp': nrm(35, (DEPTH, N_EXPERTS), 0.01),
        'w_exp_gate': nrm(36, (DEPTH, N_EXPERTS, D_MODEL, D_EXPERT), D_MODEL ** -0.5),
        'w_exp_up': nrm(37, (DEPTH, N_EXPERTS, D_MODEL, D_EXPERT), D_MODEL ** -0.5),
        'w_exp_down': nrm(38, (DEPTH, N_EXPERTS, D_EXPERT, D_MODEL), D_EXPERT ** -0.5),
    }


def reference(x_prompt, x_sample, state_lru_h, state_lru_conv, cache_cmp_k, cache_cmp_v, cache_sel_k, cache_sel_v, cache_win_k, cache_win_v, cache_dil_k, cache_dil_v, page_table, norm_mix, norm_ffn, w_in_even, conv_w, conv_b, lru_wa, lru_ba, lru_wi, lru_bi, lru_lambda, nsa_q_gain, nsa_k_gain, w_cmp_k, w_cmp_v, w_out_even, w_in_odd, dil_q_gain, dil_k_gain, w_out_odd, w_router_group, b_router_group, w_router_exp, b_router_exp, w_exp_gate, w_exp_up, w_exp_down):
    P = dict(norm_mix=norm_mix, norm_ffn=norm_ffn, w_in_even=w_in_even, conv_w=conv_w, conv_b=conv_b,
             lru_wa=lru_wa, lru_ba=lru_ba, lru_wi=lru_wi, lru_bi=lru_bi, lru_lambda=lru_lambda,
             nsa_q_gain=nsa_q_gain, nsa_k_gain=nsa_k_gain, w_cmp_k=w_cmp_k, w_cmp_v=w_cmp_v,
             w_out_even=w_out_even, w_in_odd=w_in_odd, dil_q_gain=dil_q_gain, dil_k_gain=dil_k_gain,
             w_out_odd=w_out_odd, w_router_group=w_router_group, b_router_group=b_router_group,
             w_router_exp=w_router_exp, b_router_exp=b_router_exp, w_exp_gate=w_exp_gate,
             w_exp_up=w_exp_up, w_exp_down=w_exp_down)
    B, dt = x_prompt.shape[0], x_prompt.dtype
    e_nsa = jnp.zeros((B, 0, NSA_KV, HEAD_DIM), dt)
    e_dil = jnp.zeros((B, 0, C_HEADS, HEAD_DIM), dt)
    ev0 = [(jnp.zeros((B, LRU_WIDTH), dt), jnp.zeros((B, CONV_W - 1, LRU_WIDTH), dt),
            e_nsa, e_nsa, e_nsa, e_nsa, e_nsa, e_nsa) for _ in range(N_EVEN)]
    od0 = [(e_dil, e_dil) for _ in range(N_ODD)]
    y_prompt, ev_p, od_p = trunk(x_prompt, 0, ev0, od0, P)
    p0 = page_table.shape[1] * PAGE_SIZE
    ev1 = [(state_lru_h[e], state_lru_conv[e],
            gather_pages(cache_cmp_k[e], page_table), gather_pages(cache_cmp_v[e], page_table),
            gather_pages(cache_sel_k[e], page_table), gather_pages(cache_sel_v[e], page_table),
            cache_win_k[e], cache_win_v[e]) for e in range(N_EVEN)]
    od1 = [(cache_dil_k[o], cache_dil_v[o]) for o in range(N_ODD)]
    y_sample, ev_s, od_s = trunk(x_sample, p0, ev1, od1, P)
    p_lru_h, p_lru_conv, p_cmp_k, p_cmp_v, p_sel_k, p_sel_v, p_win_k, p_win_v = stack_layers(ev_p)
    p_dil_k, p_dil_v = stack_layers(od_p)
    s_lru_h, s_lru_conv, s_cmp_k, s_cmp_v, s_sel_k, s_sel_v, s_win_k, s_win_v = stack_layers(ev_s)
    s_dil_k, s_dil_v = stack_layers(od_s)
    return (y_prompt, y_sample, p_lru_h, p_lru_conv, p_cmp_k, p_cmp_v, p_sel_k, p_sel_v, p_win_k, p_win_v, p_dil_k, p_dil_v, s_lru_h, s_lru_conv, s_cmp_k, s_cmp_v, s_sel_k, s_sel_v, s_win_k, s_win_v, s_dil_k, s_dil_v)
```

```python
import functools

import jax, jax.numpy as jnp
from jax import lax
import numpy as np
from jax.experimental import pallas as pl
from jax.experimental.pallas import tpu as pltpu

D_MODEL = 1024
BATCH = 2
SEQ = 8192
DEPTH = 2
DEC_BATCH = 128
DEC_SEQ = 4
PAST_LEN = 2048
PAGE_SIZE = 128

HEAD_DIM = 64
LRU_WIDTH = D_MODEL // 2
LRU_BLOCKS = LRU_WIDTH // HEAD_DIM
LRU_BW = LRU_WIDTH // LRU_BLOCKS
CONV_W = 4
LRU_C = 8.0
NSA_HEADS = (D_MODEL // 2) // HEAD_DIM
NSA_KV = 2
NSA_GROUP = NSA_HEADS // NSA_KV
CMP_LEN = 32
SEL_LEN = 64
CMP_PER_SEL = SEL_LEN // CMP_LEN
N_SEL = 16
SW_WIN = 512
FORCE = 1000.0
C_HEADS = D_MODEL // HEAD_DIM
DIL_PATTERNS = ((128, 1), (512, 4), (2048, 16))
C_WINDOW = 2048
N_GROUPS = 4
EXP_PER_GROUP = 8
N_EXPERTS = N_GROUPS * EXP_PER_GROUP
D_EXPERT = D_MODEL // 2
TOP_K = 2
MOE_BLK = 128
Q_BLOCK = 128
ROPE_THETA = 10000.0
EPS = 1e-6
NEG = -1e30
TINY = 1e-30
N_EVEN = (DEPTH + 1) // 2
N_ODD = DEPTH // 2
EVEN_SIZES = (LRU_WIDTH, LRU_WIDTH, NSA_HEADS * HEAD_DIM) + (NSA_KV * HEAD_DIM,) * 6 + (3 * NSA_HEADS,)
EVEN_COLS = sum(EVEN_SIZES)
EVEN_OUT = LRU_WIDTH + NSA_HEADS * HEAD_DIM
C_WIDTH = C_HEADS * HEAD_DIM


def _mm_body(x_ref, w_ref, o_ref):
    o_ref[...] = jnp.dot(x_ref[...].astype(jnp.bfloat16), w_ref[...].astype(jnp.bfloat16),
                         preferred_element_type=jnp.float32)


def matmul(x, w, tm=512, tn=512):
    lead = x.shape[:-1]
    K = x.shape[-1]
    M = w.shape[-1]
    x2 = x.reshape(-1, K)
    N = x2.shape[0]
    tm = min(tm, N)
    tn = min(tn, M)
    out = pl.pallas_call(
        _mm_body,
        grid=(pl.cdiv(N, tm), pl.cdiv(M, tn)),
        in_specs=[pl.BlockSpec((tm, K), lambda i, j: (i, 0)),
                  pl.BlockSpec((K, tn), lambda i, j: (0, j))],
        out_specs=pl.BlockSpec((tm, tn), lambda i, j: (i, j)),
        out_shape=jax.ShapeDtypeStruct((N, M), jnp.float32),
        name="matmul",
    )(x2, w)
    return out.reshape(lead + (M,))


def rms_norm(x, g):
    xf = x.astype(jnp.float32)
    y = xf * lax.rsqrt(jnp.mean(xf * xf, axis=-1, keepdims=True) + EPS)
    return (y * g.astype(jnp.float32)).astype(x.dtype)


def rope(x, pos):
    half = HEAD_DIM // 2
    freq = ROPE_THETA ** (-jnp.arange(half, dtype=jnp.float32) / half)
    ang = pos.astype(jnp.float32)[:, None] * freq[None, :]
    cos = jnp.cos(ang)[None, :, None, :]
    sin = jnp.sin(ang)[None, :, None, :]
    xf = x.astype(jnp.float32)
    x1, x2 = xf[..., :half], xf[..., half:]
    return jnp.concatenate([x1 * cos - x2 * sin, x2 * cos + x1 * sin], axis=-1).astype(x.dtype)


def masked_softmax(s, mask):
    s = jnp.where(mask, s.astype(jnp.float32), NEG)
    m = jnp.max(s, axis=-1, keepdims=True)
    e = jnp.where(mask, jnp.exp(s - m), 0.0)
    den = jnp.maximum(jnp.sum(e, axis=-1, keepdims=True), TINY)
    return e / den, m + jnp.log(den)


def split_cols(z, sizes):
    out, o = [], 0
    for s in sizes:
        out.append(z[..., o:o + s])
        o += s
    return out


def over_query_blocks(fn, T):
    if T > Q_BLOCK and T % Q_BLOCK == 0:
        out = lax.map(lambda i: fn(i * Q_BLOCK, Q_BLOCK), jnp.arange(T // Q_BLOCK))
        return jnp.swapaxes(out, 0, 1).reshape(out.shape[1], T, out.shape[-1])
    return fn(0, T)


def gather_pages(pool, page_table):
    g = pool[page_table]
    return g.reshape(g.shape[0], g.shape[1] * g.shape[2], g.shape[3], g.shape[4])


def causal_conv(u, buf, w, b):
    T = u.shape[1]
    up = jnp.concatenate([buf, u], axis=1)
    y = b + up[:, 0:T] * w[0]
    for j in range(1, CONV_W):
        y = y + up[:, j:j + T] * w[j]
    return y, up[:, T:]


def _lin_comb(l, r):
    return (l[0] * r[0], r[0] * l[1] + r[1])


def rg_lru(u, h0, wa, ba, wi, bi, lam):
    B, T, W = u.shape
    ub = u.reshape(B, T, LRU_BLOCKS, LRU_BW)
    r = jax.nn.sigmoid((jnp.einsum('btki,kij->btkj', ub, wa).reshape(B, T, W) + ba).astype(jnp.float32))
    gi = jax.nn.sigmoid((jnp.einsum('btki,kij->btkj', ub, wi).reshape(B, T, W) + bi).astype(jnp.float32))
    log_a = -LRU_C * r * jax.nn.softplus(-lam.astype(jnp.float32))
    a = jnp.exp(log_a)
    b = jnp.sqrt(-jnp.expm1(2.0 * log_a)) * (gi * u.astype(jnp.float32))
    b = b.at[:, 0].add(a[:, 0] * h0.astype(jnp.float32))
    _, h = lax.associative_scan(_lin_comb, (a, b), axis=1)
    return h.astype(u.dtype), h[:, -1].astype(u.dtype)


def compress(rows, w, t_pad):
    B, T = rows.shape[:2]
    r = jnp.pad(rows, ((0, 0), (0, t_pad - T), (0, 0), (0, 0)))
    r = r.reshape(B, t_pad // CMP_LEN, CMP_LEN, NSA_KV, HEAD_DIM)
    return jnp.einsum('bnjgd,jde->bnge', r, w)


def sel_blocks(rows, t_pad):
    B, T = rows.shape[:2]
    r = jnp.pad(rows, ((0, 0), (0, t_pad - T), (0, 0), (0, 0)))
    return r.reshape(B, t_pad // SEL_LEN, SEL_LEN, NSA_KV, HEAD_DIM).transpose(0, 3, 1, 2, 4)


def nsa_attend(q, q_rot, gates, qpos, wstart, kc, vc, ksg, vsg, kw_p, vw_p):
    B, Q = q.shape[:2]
    G, M = NSA_KV, NSA_GROUP
    scale = HEAD_DIM ** -0.5
    qg = q.reshape(B, Q, G, M, HEAD_DIM)
    qr = q_rot.reshape(B, Q, G, M, HEAD_DIM)
    ncb = kc.shape[1]
    s_c = jnp.einsum('bqgmd,bngd->bgmqn', qg, kc) * scale
    cmask = ((jnp.arange(ncb) + 1) * CMP_LEN - 1)[None, :] <= qpos[:, None]
    p_c, _ = masked_softmax(s_c, cmask)
    o_c = jnp.einsum('bgmqn,bngd->bqgmd', p_c.astype(vc.dtype), vc)
    nsb = ksg.shape[2]
    imp = p_c.sum(axis=2).reshape(B, G, Q, nsb, CMP_PER_SEL).sum(-1)
    sb = jnp.arange(nsb)[None, :]
    cur = (qpos // SEL_LEN)[:, None]
    valid = sb * SEL_LEN <= qpos[:, None]
    forced = (sb == 0) | (sb == cur) | (sb == cur - 1)
    score = jnp.where(valid, imp + jnp.where(forced, FORCE, 0.0), NEG)
    top_s, top_i = lax.top_k(score, min(N_SEL, nsb))
    n = top_i.shape[-1]
    bi = jnp.arange(B)[:, None, None, None]
    gi = jnp.arange(G)[None, :, None, None]
    k_sel = ksg[bi, gi, top_i]
    v_sel = vsg[bi, gi, top_i]
    kpos = top_i[..., None] * SEL_LEN + jnp.arange(SEL_LEN)
    smask = (top_s > 0.5 * NEG)[..., None] & (kpos <= qpos[None, None, :, None, None])
    s_s = jnp.einsum('bqgmd,bgqnld->bgmqnl', qr, k_sel).reshape(B, G, M, Q, n * SEL_LEN) * scale
    p_s, _ = masked_softmax(s_s, smask.reshape(B, G, 1, Q, n * SEL_LEN))
    o_s = jnp.einsum('bgmqk,bgqkd->bqgmd', p_s.astype(v_sel.dtype), v_sel.reshape(B, G, Q, n * SEL_LEN, HEAD_DIM))
    span = SW_WIN + Q
    kwb = lax.dynamic_slice_in_dim(kw_p, wstart, span, axis=1)
    vwb = lax.dynamic_slice_in_dim(vw_p, wstart, span, axis=1)
    qloc = wstart + jnp.arange(Q)
    kloc = wstart - SW_WIN + jnp.arange(span)
    dist = qloc[:, None] - kloc[None, :]
    wmask = (dist >= 0) & (dist <= SW_WIN) & (kloc[None, :] >= 0)
    s_w = jnp.einsum('bqgmd,bkgd->bgmqk', qr, kwb) * scale
    p_w, _ = masked_softmax(s_w, wmask)
    o_w = jnp.einsum('bgmqk,bkgd->bqgmd', p_w.astype(vwb.dtype), vwb)
    gt = gates.reshape(B, Q, G, M, 3).astype(o_c.dtype)
    o = gt[..., 0:1] * o_c + gt[..., 1:2] * o_s + gt[..., 2:3] * o_w
    return o.reshape(B, Q, NSA_HEADS * HEAD_DIM)


def even_mixer(h, p0, past, P, e):
    lru_h0, conv_buf, past_ck, past_cv, past_sk, past_sv, win_k, win_v = past
    B, T, _ = h.shape
    z = matmul(h, P['w_in_even'][e])
    u, y, q, ck, cv, sk, sv, wk, wv, g = split_cols(z, EVEN_SIZES)
    uc, conv_new = causal_conv(u, conv_buf, P['conv_w'][e], P['conv_b'][e])
    hl, h_last = rg_lru(uc, lru_h0, P['lru_wa'][e], P['lru_ba'][e], P['lru_wi'][e], P['lru_bi'][e], P['lru_lambda'][e])
    lru_out = hl * jax.nn.gelu(y)
    pos = p0 + jnp.arange(T)
    kg = P['nsa_k_gain'][e]
    kvs = (B, T, NSA_KV, HEAD_DIM)
    q = rms_norm(q.reshape(B, T, NSA_HEADS, HEAD_DIM), P['nsa_q_gain'][e])
    q_rot = rope(q, pos)
    ck = rms_norm(ck.reshape(kvs), kg[0])
    cv = cv.reshape(kvs)
    sk = rope(rms_norm(sk.reshape(kvs), kg[1]), pos)
    sv = sv.reshape(kvs)
    wk = rope(rms_norm(wk.reshape(kvs), kg[2]), pos)
    wv = wv.reshape(kvs)
    gates = jax.nn.sigmoid(g.astype(jnp.float32)).reshape(B, T, NSA_HEADS, 3)
    ck_all = jnp.concatenate([past_ck, ck], axis=1)
    cv_all = jnp.concatenate([past_cv, cv], axis=1)
    sk_all = jnp.concatenate([past_sk, sk], axis=1)
    sv_all = jnp.concatenate([past_sv, sv], axis=1)
    t_k = ck_all.shape[1]
    t_pad = -(-t_k // SEL_LEN) * SEL_LEN
    kc = compress(ck_all, P['w_cmp_k'][e], t_pad)
    vc = compress(cv_all, P['w_cmp_v'][e], t_pad)
    ksg = sel_blocks(sk_all, t_pad)
    vsg = sel_blocks(sv_all, t_pad)
    wk_all = jnp.concatenate([win_k, wk], axis=1)
    wv_all = jnp.concatenate([win_v, wv], axis=1)
    wb = win_k.shape[1]
    pad_w = ((0, 0), (SW_WIN, 0), (0, 0), (0, 0))
    wk_p = jnp.pad(wk_all, pad_w)
    wv_p = jnp.pad(wv_all, pad_w)

    def block(q0, qb):
        sl = lambda a: lax.dynamic_slice_in_dim(a, q0, qb, axis=1)
        qpos = p0 + q0 + jnp.arange(qb)
        return nsa_attend(sl(q), sl(q_rot), sl(gates), qpos, wb + q0, kc, vc, ksg, vsg, wk_p, wv_p)

    nsa_out = over_query_blocks(block, T)
    mix = matmul(jnp.concatenate([lru_out, nsa_out.astype(lru_out.dtype)], axis=-1), P['w_out_even'][e])
    keep = min(SW_WIN, wk_all.shape[1])
    return mix, (h_last, conv_new, ck, cv, sk, sv, wk_all[:, -keep:], wv_all[:, -keep:])


def dilated_attend(q, kp, vp, qloc):
    scale = HEAD_DIM ** -0.5
    outs, lses = [], []
    for w, d in DIL_PATTERNS:
        kl = qloc[:, None] - d * jnp.arange(w // d + 1)[None, :]
        kg = kp[:, kl + C_WINDOW]
        vg = vp[:, kl + C_WINDOW]
        s = jnp.einsum('bqhd,bqnhd->bhqn', q, kg) * scale
        p, lse = masked_softmax(s, kl >= 0)
        outs.append(jnp.einsum('bhqn,bqnhd->bqhd', p.astype(vg.dtype), vg))
        lses.append(lse)
    wts = jax.nn.softmax(jnp.stack(lses), axis=0)
    o = jnp.swapaxes(wts[0], 1, 2).astype(outs[0].dtype) * outs[0]
    for i in range(1, len(DIL_PATTERNS)):
        o = o + jnp.swapaxes(wts[i], 1, 2).astype(outs[i].dtype) * outs[i]
    return o.reshape(o.shape[0], o.shape[1], C_WIDTH)


def odd_mixer(h, p0, past, P, o):
    buf_k, buf_v = past
    B, T, _ = h.shape
    z = matmul(h, P['w_in_odd'][o])
    q, k, v = split_cols(z, (C_WIDTH,) * 3)
    shp = (B, T, C_HEADS, HEAD_DIM)
    pos = p0 + jnp.arange(T)
    q = rope(rms_norm(q.reshape(shp), P['dil_q_gain'][o]), pos)
    k = rope(rms_norm(k.reshape(shp), P['dil_k_gain'][o]), pos)
    v = v.reshape(shp)
    k_all = jnp.concatenate([buf_k, k], axis=1)
    v_all = jnp.concatenate([buf_v, v], axis=1)
    wb = buf_k.shape[1]
    pad = ((0, 0), (C_WINDOW, 0), (0, 0), (0, 0))
    kp = jnp.pad(k_all, pad)
    vp = jnp.pad(v_all, pad)

    def block(q0, qb):
        return dilated_attend(lax.dynamic_slice_in_dim(q, q0, qb, axis=1), kp, vp, wb + q0 + jnp.arange(qb))

    att = over_query_blocks(block, T)
    keep = min(C_WINDOW, k_all.shape[1])
    return matmul(att, P['w_out_odd'][o]), (k_all[:, -keep:], v_all[:, -keep:])


def routed_experts(xf, eid, wts, w_gate, w_up, w_down):
    N, D = xf.shape
    M = N * TOP_K
    e_flat = eid.reshape(M)
    tok = jnp.repeat(jnp.arange(N, dtype=jnp.int32), TOP_K)
    w_flat = wts.reshape(M)
    order = jnp.argsort(e_flat)
    e_sorted = e_flat[order]
    counts = jnp.bincount(e_flat, length=N_EXPERTS)
    padded = (counts + MOE_BLK - 1) // MOE_BLK * MOE_BLK
    start = jnp.cumsum(counts) - counts
    pend = jnp.cumsum(padded)
    pstart = pend - padded
    dest = pstart[e_sorted] + (jnp.arange(M) - start[e_sorted])
    n_blocks = -(-(M + N_EXPERTS * (MOE_BLK - 1)) // MOE_BLK)
    n_slots = n_blocks * MOE_BLK
    slot_tok = jnp.full((n_slots,), N, jnp.int32).at[dest].set(tok[order])
    slot_w = jnp.zeros((n_slots,), xf.dtype).at[dest].set(w_flat[order].astype(xf.dtype))
    blk_e = jnp.minimum(jnp.searchsorted(pend, jnp.arange(n_blocks) * MOE_BLK, side='right'), N_EXPERTS - 1)
    x_pad = jnp.concatenate([xf, jnp.zeros((1, D), xf.dtype)], axis=0)

    def run(args):
        toks, w, e = args
        xb = x_pad[toks]
        hid = jax.nn.silu(xb @ w_gate[e]) * (xb @ w_up[e])
        return (hid @ w_down[e]) * w[:, None]

    yb = lax.map(run, (slot_tok.reshape(n_blocks, MOE_BLK), slot_w.reshape(n_blocks, MOE_BLK), blk_e))
    y = jnp.zeros((N + 1, D), xf.dtype).at[slot_tok].add(yb.reshape(n_slots, D))
    return y[:N]


def hier_moe(x, P, layer):
    B, T, D = x.shape
    N = B * T
    xf = x.reshape(N, D)
    rows = jnp.arange(N)
    g_logit = (xf @ P['w_router_group'][layer] + P['b_router_group'][layer]).astype(jnp.float32)
    g_top = jnp.argmax(g_logit, axis=-1)
    g_w = jax.nn.softmax(g_logit, axis=-1)[rows, g_top]
    e_logit = (xf @ P['w_router_exp'][layer] + P['b_router_exp'][layer]).astype(jnp.float32)
    e_logit = e_logit.reshape(N, N_GROUPS, EXP_PER_GROUP)[rows, g_top]
    top_l, top_i = lax.top_k(e_logit, TOP_K)
    top_p = jax.nn.softmax(top_l, axis=-1)
    eid = (g_top[:, None] * EXP_PER_GROUP + top_i).astype(jnp.int32)
    wts = g_w[:, None] * top_p
    y = routed_experts(xf, eid, wts, P['w_exp_gate'][layer], P['w_exp_up'][layer], P['w_exp_down'][layer])
    return y.reshape(B, T, D)


def trunk(x, p0, even_past, odd_past, P):
    even_new, odd_new = [], []
    for layer in range(DEPTH):
        hn = rms_norm(x, P['norm_mix'][layer])
        if layer % 2 == 0:
            mix, st = even_mixer(hn, p0, even_past[layer // 2], P, layer // 2)
            even_new.append(st)
        else:
            mix, st = odd_mixer(hn, p0, odd_past[layer // 2], P, layer // 2)
            odd_new.append(st)
        x = x + mix.astype(x.dtype)
        x = x + hier_moe(rms_norm(x, P['norm_ffn'][layer]), P, layer).astype(x.dtype)
    return x, even_new, odd_new


def stack_layers(states):
    return [jnp.stack(items) for items in zip(*states)]


def kernel(x_prompt, x_sample, state_lru_h, state_lru_conv, cache_cmp_k, cache_cmp_v, cache_sel_k, cache_sel_v, cache_win_k, cache_win_v, cache_dil_k, cache_dil_v, page_table, norm_mix, norm_ffn, w_in_even, conv_w, conv_b, lru_wa, lru_ba, lru_wi, lru_bi, lru_lambda, nsa_q_gain, nsa_k_gain, w_cmp_k, w_cmp_v, w_out_even, w_in_odd, dil_q_gain, dil_k_gain, w_out_odd, w_router_group, b_router_group, w_router_exp, b_router_exp, w_exp_gate, w_exp_up, w_exp_down):
    P = dict(norm_mix=norm_mix, norm_ffn=norm_ffn, w_in_even=w_in_even, conv_w=conv_w, conv_b=conv_b,
             lru_wa=lru_wa, lru_ba=lru_ba, lru_wi=lru_wi, lru_bi=lru_bi, lru_lambda=lru_lambda,
             nsa_q_gain=nsa_q_gain, nsa_k_gain=nsa_k_gain, w_cmp_k=w_cmp_k, w_cmp_v=w_cmp_v,
             w_out_even=w_out_even, w_in_odd=w_in_odd, dil_q_gain=dil_q_gain, dil_k_gain=dil_k_gain,
             w_out_odd=w_out_odd, w_router_group=w_router_group, b_router_group=b_router_group,
             w_router_exp=w_router_exp, b_router_exp=b_router_exp, w_exp_gate=w_exp_gate,
             w_exp_up=w_exp_up, w_exp_down=w_exp_down)
    B, dt = x_prompt.shape[0], x_prompt.dtype
    e_nsa = jnp.zeros((B, 0, NSA_KV, HEAD_DIM), dt)
    e_dil = jnp.zeros((B, 0, C_HEADS, HEAD_DIM), dt)
    ev0 = [(jnp.zeros((B, LRU_WIDTH), dt), jnp.zeros((B, CONV_W - 1, LRU_WIDTH), dt),
            e_nsa, e_nsa, e_nsa, e_nsa, e_nsa, e_nsa) for _ in range(N_EVEN)]
    od0 = [(e_dil, e_dil) for _ in range(N_ODD)]
    y_prompt, ev_p, od_p = trunk(x_prompt, 0, ev0, od0, P)
    p0 = page_table.shape[1] * PAGE_SIZE
    ev1 = [(state_lru_h[e], state_lru_conv[e],
            gather_pages(cache_cmp_k[e], page_table), gather_pages(cache_cmp_v[e], page_table),
            gather_pages(cache_sel_k[e], page_table), gather_pages(cache_sel_v[e], page_table),
            cache_win_k[e], cache_win_v[e]) for e in range(N_EVEN)]
    od1 = [(cache_dil_k[o], cache_dil_v[o]) for o in range(N_ODD)]
    y_sample, ev_s, od_s = trunk(x_sample, p0, ev1, od1, P)
    p_lru_h, p_lru_conv, p_cmp_k, p_cmp_v, p_sel_k, p_sel_v, p_win_k, p_win_v = stack_layers(ev_p)
    p_dil_k, p_dil_v = stack_layers(od_p)
    s_lru_h, s_lru_conv, s_cmp_k, s_cmp_v, s_sel_k, s_sel_v, s_win_k, s_win_v = stack_layers(ev_s)
    s_dil_k, s_dil_v = stack_layers(od_s)
    return (y_prompt, y_sample, p_lru_h, p_lru_conv, p_cmp_k, p_cmp_v, p_sel_k, p_sel_v, p_win_k, p_win_v, p_dil_k, p_dil_v, s_lru_h, s_lru_conv, s_cmp_k, s_cmp_v, s_sel_k, s_sel_v, s_win_k, s_win_v, s_dil_k, s_dil_v)
```

```python
import functools

import jax, jax.numpy as jnp
from jax import lax
import numpy as np
from jax.experimental import pallas as pl
from jax.experimental.pallas import tpu as pltpu

D_MODEL = 1024
BATCH = 2
SEQ = 8192
DEPTH = 2
DEC_BATCH = 128
DEC_SEQ = 4
PAST_LEN = 2048
PAGE_SIZE = 128

HEAD_DIM = 64
LRU_WIDTH = D_MODEL // 2
LRU_BLOCKS = LRU_WIDTH // HEAD_DIM
LRU_BW = LRU_WIDTH // LRU_BLOCKS
CONV_W = 4
LRU_C = 8.0
NSA_HEADS = (D_MODEL // 2) // HEAD_DIM
NSA_KV = 2
NSA_GROUP = NSA_HEADS // NSA_KV
CMP_LEN = 32
SEL_LEN = 64
CMP_PER_SEL = SEL_LEN // CMP_LEN
N_SEL = 16
SW_WIN = 512
FORCE = 1000.0
C_HEADS = D_MODEL // HEAD_DIM
DIL_PATTERNS = ((128, 1), (512, 4), (2048, 16))
C_WINDOW = 2048
N_GROUPS = 4
EXP_PER_GROUP = 8
N_EXPERTS = N_GROUPS * EXP_PER_GROUP
D_EXPERT = D_MODEL // 2
TOP_K = 2
MOE_BLK = 128
Q_BLOCK = 128
ROPE_THETA = 10000.0
EPS = 1e-6
NEG = -1e30
TINY = 1e-30
N_EVEN = (DEPTH + 1) // 2
N_ODD = DEPTH // 2
EVEN_SIZES = (LRU_WIDTH, LRU_WIDTH, NSA_HEADS * HEAD_DIM) + (NSA_KV * HEAD_DIM,) * 6 + (3 * NSA_HEADS,)
EVEN_COLS = sum(EVEN_SIZES)
EVEN_OUT = LRU_WIDTH + NSA_HEADS * HEAD_DIM
C_WIDTH = C_HEADS * HEAD_DIM


def _mm_body(x_ref, w_ref, o_ref):
    o_ref[...] = jnp.dot(x_ref[...].astype(jnp.bfloat16), w_ref[...].astype(jnp.bfloat16),
                         preferred_element_type=jnp.float32)


def matmul(x, w, tm=512, tn=512):
    lead = x.shape[:-1]
    K = x.shape[-1]
    M = w.shape[-1]
    x2 = x.reshape(-1, K)
    N = x2.shape[0]
    tm = min(tm, N)
    tn = min(tn, M)
    out = pl.pallas_call(
        _mm_body,
        grid=(pl.cdiv(N, tm), pl.cdiv(M, tn)),
        in_specs=[pl.BlockSpec((tm, K), lambda i, j: (i, 0)),
                  pl.BlockSpec((K, tn), lambda i, j: (0, j))],
        out_specs=pl.BlockSpec((tm, tn), lambda i, j: (i, j)),
        out_shape=jax.ShapeDtypeStruct((N, M), jnp.float32),
        name="matmul",
    )(x2, w)
    return out.reshape(lead + (M,))


LANES = 128
VMEM_LIMIT = 48 * 1024 * 1024
BF16 = jnp.bfloat16
F32 = jnp.float32


def _cparams(*sem):
    return pltpu.CompilerParams(dimension_semantics=sem, vmem_limit_bytes=VMEM_LIMIT)


def _rope_tables(pos):
    half = HEAD_DIM // 2
    freq = ROPE_THETA ** (-jnp.arange(half, dtype=F32) / half)
    ang = pos.astype(F32)[:, None] * freq[None, :]
    cos, sin = jnp.cos(ang), jnp.sin(ang)
    return jnp.tile(cos, (1, 4)), jnp.tile(jnp.concatenate([-sin, sin], axis=1), (1, 2))


def _head_mean_matrix(width):
    h = jnp.arange(width) // HEAD_DIM
    return jnp.where(h[:, None] == h[None, :], 1.0 / HEAD_DIM, 0.0).astype(BF16)


def _head_norm(x, gain, gmat):
    x2 = x * x
    hi = x2.astype(BF16)
    lo = (x2 - hi.astype(F32)).astype(BF16)
    ms = jnp.dot(hi, gmat, preferred_element_type=F32) + jnp.dot(lo, gmat, preferred_element_type=F32)
    return x * lax.rsqrt(ms + EPS) * gain


def _rope_lanes(y, cos, sin):
    hi_half = (lax.broadcasted_iota(jnp.int32, (y.shape[0], LANES), 1) & (HEAD_DIM // 2)) != 0
    outs = []
    for c in range(y.shape[1] // LANES):
        yc = y[:, c * LANES:(c + 1) * LANES]
        partner = jnp.where(hi_half, pltpu.roll(yc, HEAD_DIM // 2, 1), pltpu.roll(yc, LANES - HEAD_DIM // 2, 1))
        outs.append(yc * cos + partner * sin)
    return outs[0] if len(outs) == 1 else jnp.concatenate(outs, axis=1)


def _inproj_odd_body(x_ref, w_ref, cos_ref, sin_ref, gain_ref, gmat_ref, o_ref, *, n_rope):
    j = pl.program_id(1)
    acc = jnp.dot(x_ref[...].astype(BF16), w_ref[...].astype(BF16), preferred_element_type=F32)

    @pl.when(j < n_rope)
    def _():
        o_ref[...] = _rope_lanes(_head_norm(acc, gain_ref[0], gmat_ref[...]), cos_ref[...], sin_ref[...])

    @pl.when(j >= n_rope)
    def _():
        o_ref[...] = acc


def inproj_odd(h2, w, q_gain, k_gain, cos, sin, tm=512, tn=512):
    N, K = h2.shape
    M = w.shape[-1]
    tm = min(tm, N)
    reps = tn // HEAD_DIM
    per = C_WIDTH // tn
    gains = jnp.concatenate([jnp.tile(jnp.tile(q_gain, reps)[None], (per, 1)),
                             jnp.tile(jnp.tile(k_gain, reps)[None], (per, 1)),
                             jnp.ones((per, tn), F32)], axis=0)[:, None, :]
    return pl.pallas_call(
        functools.partial(_inproj_odd_body, n_rope=2 * per),
        grid=(N // tm, M // tn),
        in_specs=[pl.BlockSpec((tm, K), lambda i, j: (i, 0)),
                  pl.BlockSpec((K, tn), lambda i, j: (0, j)),
                  pl.BlockSpec((tm, LANES), lambda i, j: (i, 0)),
                  pl.BlockSpec((tm, LANES), lambda i, j: (i, 0)),
                  pl.BlockSpec((1, 1, tn), lambda i, j: (j, 0, 0)),
                  pl.BlockSpec((tn, tn), lambda i, j: (0, 0))],
        out_specs=pl.BlockSpec((tm, tn), lambda i, j: (i, j)),
        out_shape=jax.ShapeDtypeStruct((N, M), F32),
        compiler_params=_cparams("parallel", "arbitrary"),
        name="inproj_odd",
    )(h2, w, cos, sin, gains, _head_mean_matrix(tn))


DIL_TQ = 128


def _dil_body(q_ref, kc_ref, kp_ref, vc_ref, vp_ref, o_ref, lse_ref):
    i = pl.program_id(2)
    tq = q_ref.shape[0]
    row = lax.broadcasted_iota(jnp.int32, (tq, tq), 0)
    col = lax.broadcasted_iota(jnp.int32, (tq, tq), 1)
    mask_cur = col <= row
    mask_prev = jnp.logical_and(col >= row, i > 0)
    low = lax.broadcasted_iota(jnp.int32, (tq, LANES), 1) < HEAD_DIM
    scale = HEAD_DIM ** -0.5
    dn = (((1,), (1,)), ((), ()))
    for hp in range(q_ref.shape[1] // LANES):
        sl = slice(hp * LANES, (hp + 1) * LANES)
        q2 = q_ref[:, sl] * scale
        kc = kc_ref[:, sl].astype(BF16)
        kp = kp_ref[:, sl].astype(BF16)
        vc = vc_ref[:, sl].astype(BF16)
        vp = vp_ref[:, sl].astype(BF16)
        outs, lses = [], []
        for half in range(2):
            qh = jnp.where(low if half == 0 else jnp.logical_not(low), q2, 0.0).astype(BF16)
            s_c = jnp.where(mask_cur, lax.dot_general(qh, kc, dn, preferred_element_type=F32), NEG)
            s_p = jnp.where(mask_prev, lax.dot_general(qh, kp, dn, preferred_element_type=F32), NEG)
            m = jnp.maximum(jnp.max(s_c, axis=-1, keepdims=True), jnp.max(s_p, axis=-1, keepdims=True))
            e_c = jnp.exp(s_c - m)
            e_p = jnp.exp(s_p - m)
            den = jnp.maximum(jnp.sum(e_c, axis=-1, keepdims=True) + jnp.sum(e_p, axis=-1, keepdims=True), TINY)
            o = (jnp.dot(e_c.astype(BF16), vc, preferred_element_type=F32)
                 + jnp.dot(e_p.astype(BF16), vp, preferred_element_type=F32))
            outs.append(o / den)
            lses.append(jnp.broadcast_to(m + jnp.log(den), (tq, LANES)))
        o_ref[:, sl] = jnp.where(low, outs[0], outs[1])
        lse_ref[:, sl] = jnp.where(low, lses[0], lses[1])


def dilated_pattern(z, B, T, d):
    W = C_WIDTH
    Td = T // d
    zv = z.reshape(B, Td, d * 3 * W)
    blk = lambda off, prev: pl.BlockSpec(
        (None, DIL_TQ, W), (lambda b, r, i: (b, jnp.maximum(i - 1, 0), 3 * r + off)) if prev
        else (lambda b, r, i: (b, i, 3 * r + off)))
    o_spec = pl.BlockSpec((None, DIL_TQ, W), lambda b, r, i: (b, i, r))
    out, lse = pl.pallas_call(
        _dil_body,
        grid=(B, d, Td // DIL_TQ),
        in_specs=[blk(0, False), blk(1, False), blk(1, True), blk(2, False), blk(2, True)],
        out_specs=[o_spec, o_spec],
        out_shape=[jax.ShapeDtypeStruct((B, Td, d * W), F32)] * 2,
        compiler_params=_cparams("parallel", "parallel", "arbitrary"),
        name="dilated_d%d" % d,
    )(zv, zv, zv, zv, zv)
    return out.reshape(B * T, W), lse.reshape(B * T, W)


def _combine_outproj_body(o1, o2, o3, l1, l2, l3, w_ref, out_ref):
    m = jnp.maximum(jnp.maximum(l1[...], l2[...]), l3[...])
    e1, e2, e3 = jnp.exp(l1[...] - m), jnp.exp(l2[...] - m), jnp.exp(l3[...] - m)
    den = e1 + e2 + e3
    att = (e1 / den) * o1[...] + (e2 / den) * o2[...] + (e3 / den) * o3[...]
    out_ref[...] = jnp.dot(att.astype(BF16), w_ref[...].astype(BF16), preferred_element_type=F32)


def combine_outproj(outs, lses, w, tm=256):
    N, W = outs[0].shape
    M = w.shape[-1]
    row = pl.BlockSpec((tm, W), lambda i: (i, 0))
    return pl.pallas_call(
        _combine_outproj_body,
        grid=(N // tm,),
        in_specs=[row] * 6 + [pl.BlockSpec((W, M), lambda i: (0, 0))],
        out_specs=pl.BlockSpec((tm, M), lambda i: (i, 0)),
        out_shape=jax.ShapeDtypeStruct((N, M), F32),
        compiler_params=_cparams("parallel"),
        name="dil_combine_outproj",
    )(*outs, *lses, w)


def odd_mixer_prompt(h, P, o):
    B, T, D = h.shape
    assert all(w // d == DIL_TQ and T % (d * DIL_TQ) == 0 for w, d in DIL_PATTERNS)
    cos, sin = _rope_tables(jnp.tile(jnp.arange(T), B))
    z = inproj_odd(h.reshape(B * T, D), P['w_in_odd'][o], P['dil_q_gain'][o], P['dil_k_gain'][o], cos, sin)
    res = [dilated_pattern(z, B, T, d) for _, d in DIL_PATTERNS]
    mix = combine_outproj([r[0] for r in res], [r[1] for r in res], P['w_out_odd'][o])
    keep = min(C_WINDOW, T)
    shp = (B, T, C_HEADS, HEAD_DIM)
    k = z[:, C_WIDTH:2 * C_WIDTH].reshape(shp)
    v = z[:, 2 * C_WIDTH:].reshape(shp)
    return mix.reshape(B, T, D), (k[:, -keep:], v[:, -keep:])


EVEN_PAD = -(-EVEN_COLS // LANES) * LANES
NSA_TQ = 128
NSA_TK = 512
NSA_SPAN = SW_WIN + NSA_TQ
NSA_MAXBLK = LANES
SEL_OFF = -1e9
M_INIT = -3e38


def _dup_group(x, low):
    r = pltpu.roll(x, HEAD_DIM, 1)
    return jnp.where(low, x, r), jnp.where(low, r, x)


def _nsa_prep_body(zq_ref, zkv_ref, zg_ref, cos_ref, sin_ref, qg_ref, kg_ref, gmat_ref,
                   qn_ref, qr_ref, ck_ref, sk_ref, wk_ref, skd_ref, svd_ref, wkd_ref, wvd_ref, gt_ref):
    cos, sin = cos_ref[...], sin_ref[...]
    low = lax.broadcasted_iota(jnp.int32, (zq_ref.shape[0], LANES), 1) < HEAD_DIM
    qn = _head_norm(zq_ref[...], qg_ref[...], gmat_ref[...])
    qn_ref[...] = qn
    qr_ref[...] = _rope_lanes(qn, cos, sin)
    g1 = gmat_ref[0:LANES, 0:LANES]
    col = lambda c: zkv_ref[:, c * LANES:(c + 1) * LANES]
    ck_ref[...] = _head_norm(col(0), kg_ref[0:1, :], g1)
    sk = _rope_lanes(_head_norm(col(2), kg_ref[1:2, :], g1), cos, sin)
    wk = _rope_lanes(_head_norm(col(4), kg_ref[2:3, :], g1), cos, sin)
    sk_ref[...] = sk
    wk_ref[...] = wk
    for x, ref in ((sk, skd_ref), (col(3), svd_ref), (wk, wkd_ref), (col(5), wvd_ref)):
        d0, d1 = _dup_group(x, low)
        ref[0] = d0.astype(BF16)
        ref[1] = d1.astype(BF16)
    gt = 1.0 / (1.0 + jnp.exp(-zg_ref[...]))
    gt_ref[0] = gt
    gt_ref[1] = pltpu.roll(gt, LANES - 3 * NSA_GROUP, 1)


def nsa_prep(z, cos, sin, q_gain, k_gain, tm=256):
    N = z.shape[0]
    tm = min(tm, N)
    QW = NSA_HEADS * HEAD_DIM
    KW = NSA_KV * HEAD_DIM
    assert KW == LANES and QW % LANES == 0 and (LRU_WIDTH * 2) % QW == 0
    f = lambda shape: jax.ShapeDtypeStruct(shape, F32)
    b = lambda shape: jax.ShapeDtypeStruct(shape, BF16)
    row = lambda w: pl.BlockSpec((tm, w), lambda i: (i, 0))
    grp = pl.BlockSpec((NSA_KV, tm, LANES), lambda i: (0, i, 0))
    return pl.pallas_call(
        _nsa_prep_body,
        grid=(N // tm,),
        in_specs=[pl.BlockSpec((tm, QW), lambda i: (i, 2 * LRU_WIDTH // QW)),
                  pl.BlockSpec((tm, 6 * KW), lambda i: (i, (2 * LRU_WIDTH + QW) // (6 * KW))),
                  pl.BlockSpec((tm, LANES), lambda i: (i, (2 * LRU_WIDTH + QW + 6 * KW) // LANES)),
                  row(LANES), row(LANES),
                  pl.BlockSpec((1, QW), lambda i: (0, 0)),
                  pl.BlockSpec((3, LANES), lambda i: (0, 0)),
                  pl.BlockSpec((QW, QW), lambda i: (0, 0))],
        out_specs=[row(QW), row(QW), row(LANES), row(LANES), row(LANES), grp, grp, grp, grp, grp],
        out_shape=[f((N, QW)), f((N, QW)), f((N, LANES)), f((N, LANES)), f((N, LANES)),
                   b((NSA_KV, N, LANES)), b((NSA_KV, N, LANES)), b((NSA_KV, N, LANES)), b((NSA_KV, N, LANES)),
                   f((NSA_KV, N, LANES))],
        compiler_params=_cparams("parallel"),
        name="nsa_prep",
    )(z, z, z, cos, sin, jnp.tile(q_gain, NSA_HEADS)[None], jnp.tile(k_gain, (1, NSA_KV)), _head_mean_matrix(QW))


def compress_blocks(rows, w, B, T):
    nsb = T // SEL_LEN
    wc = jnp.einsum('jde,gh->jgdhe', w, jnp.eye(NSA_KV, dtype=w.dtype)).reshape(CMP_LEN * LANES, LANES)
    kc = matmul(rows.reshape(B * T // CMP_LEN, CMP_LEN * LANES), wc, tm=256)
    kc = kc.reshape(B, nsb, CMP_PER_SEL, NSA_KV, HEAD_DIM).transpose(0, 3, 2, 1, 4)
    kc = jnp.pad(kc, ((0, 0), (0, 0), (0, 0), (0, NSA_MAXBLK - nsb), (0, 0)))
    kc = kc.reshape(B, NSA_KV, CMP_PER_SEL * NSA_MAXBLK, HEAD_DIM)
    return jnp.concatenate([kc, kc], axis=-1)


def _flash_step(s, v, m_, l_, acc):
    m_new = jnp.maximum(m_, jnp.max(s, axis=-1, keepdims=True))
    alpha = jnp.exp(m_ - m_new)
    p = jnp.exp(s - m_new)
    l_new = alpha * l_ + jnp.sum(p, axis=-1, keepdims=True)
    return m_new, l_new, alpha * acc + jnp.dot(p.astype(BF16), v, preferred_element_type=F32)


def _nsa_body(qn_ref, qr_ref, gate_ref, kc_ref, vc_ref, sk_ref, sv_ref, wk_ref, wv_ref, e_ref, o_ref):
    i = pl.program_id(2)
    tq = NSA_TQ
    q0 = i * tq
    scale = HEAD_DIM ** -0.5
    dn = (((1,), (1,)), ((), ()))
    lane = lax.broadcasted_iota(jnp.int32, (tq, LANES), 1)
    low = lane < HEAD_DIM
    qpos = q0 + lax.broadcasted_iota(jnp.int32, (tq, 1), 0)

    def head(ref, m):
        pair, half = divmod(m, 2)
        x = ref[:, pair * LANES:(pair + 1) * LANES] * scale
        return jnp.where(low if half == 0 else jnp.logical_not(low), x, 0.0).astype(BF16)

    cidx = lax.broadcasted_iota(jnp.int32, (tq, CMP_PER_SEL * NSA_MAXBLK), 1)
    par = (cidx >= NSA_MAXBLK).astype(jnp.int32)
    blk = CMP_PER_SEL * (cidx - NSA_MAXBLK * par) + par
    cmask = (blk + 1) * CMP_LEN - 1 <= qpos
    kc = kc_ref[...].astype(BF16)
    vc = vc_ref[...].astype(BF16)
    psum = jnp.zeros((tq, CMP_PER_SEL * NSA_MAXBLK), F32)
    o_c = []
    for m in range(NSA_GROUP):
        s = jnp.where(cmask, lax.dot_general(head(qn_ref, m), kc, dn, preferred_element_type=F32), NEG)
        mx = jnp.max(s, axis=-1, keepdims=True)
        e = jnp.where(cmask, jnp.exp(s - mx), 0.0)
        p = e / jnp.maximum(jnp.sum(e, axis=-1, keepdims=True), TINY)
        psum = psum + p
        o_c.append(jnp.dot(p.astype(BF16), vc, preferred_element_type=F32))
    imp = psum[:, :NSA_MAXBLK] + psum[:, NSA_MAXBLK:]

    cur = lax.shift_right_logical(qpos, SEL_LEN.bit_length() - 1)
    valid = lane * SEL_LEN <= qpos
    forced = (lane == 0) | (lane == cur) | (lane == cur - 1)
    score = jnp.where(valid, imp + jnp.where(forced, FORCE, 0.0), NEG)
    lane_f = lane.astype(F32)

    def pick(_, carry):
        sc, sel = carry
        mx = jnp.max(sc, axis=-1, keepdims=True)
        first = jnp.min(jnp.where(sc == mx, lane_f, float(LANES)), axis=-1, keepdims=True)
        hit = lane_f == first
        sel = jnp.where(jnp.logical_and(hit, mx > 0.5 * NEG), 0.0, sel)
        return jnp.where(hit, M_INIT, sc), sel

    _, selneg = lax.fori_loop(0, N_SEL, pick, (score, jnp.full((tq, LANES), SEL_OFF, F32)))
    selneg = selneg.astype(BF16)

    n_full = q0 // NSA_TK
    ws = pl.multiple_of(jnp.maximum(q0 - SW_WIN, 0), NSA_TQ)
    kpos = ws + lax.broadcasted_iota(jnp.int32, (1, NSA_SPAN), 1)
    dist = qpos - kpos
    tail_mask = jnp.logical_and(kpos >= n_full * NSA_TK, dist >= 0)
    win_mask = jnp.logical_and(dist >= 0, dist <= SW_WIN)
    gates = gate_ref[...]
    outs = []
    for m in range(NSA_GROUP):
        qh = head(qr_ref, m)
        q_aug = jnp.concatenate([qh, selneg], axis=1)

        def sweep(t, carry):
            ks = pl.multiple_of(t * NSA_TK, NSA_TK)
            k_aug = jnp.concatenate([sk_ref[pl.ds(ks, NSA_TK), :], e_ref[pl.ds(ks, NSA_TK), :]], axis=1)
            s = lax.dot_general(q_aug, k_aug, dn, preferred_element_type=F32)
            return _flash_step(s, sv_ref[pl.ds(ks, NSA_TK), :], *carry)

        init = (jnp.full((tq, 1), M_INIT, F32), jnp.zeros((tq, 1), F32), jnp.zeros((tq, LANES), F32))
        carry = lax.fori_loop(0, n_full, sweep, init)
        k_aug = jnp.concatenate([sk_ref[pl.ds(ws, NSA_SPAN), :], e_ref[pl.ds(ws, NSA_SPAN), :]], axis=1)
        s = jnp.where(tail_mask, lax.dot_general(q_aug, k_aug, dn, preferred_element_type=F32), NEG)
        _, l_s, acc_s = _flash_step(s, sv_ref[pl.ds(ws, NSA_SPAN), :], *carry)
        o_s = acc_s / l_s

        s = jnp.where(win_mask, lax.dot_general(qh, wk_ref[pl.ds(ws, NSA_SPAN), :], dn, preferred_element_type=F32), NEG)
        mx = jnp.max(s, axis=-1, keepdims=True)
        e = jnp.exp(s - mx)
        den = jnp.maximum(jnp.sum(e, axis=-1, keepdims=True), TINY)
        o_w = jnp.dot(e.astype(BF16), wv_ref[pl.ds(ws, NSA_SPAN), :], preferred_element_type=F32) / den

        g = lambda j: gates[:, 3 * m + j:3 * m + j + 1]
        outs.append(g(0) * o_c[m] + g(1) * o_s + g(2) * o_w)
    for pair in range(NSA_GROUP // 2):
        o_ref[:, pair * LANES:(pair + 1) * LANES] = jnp.where(low, outs[2 * pair], outs[2 * pair + 1])


def nsa_attention(qn, qr, gates, kc, vc, skd, svd, wkd, wvd, B, T):
    assert T % NSA_TK == 0 and NSA_SPAN <= T <= NSA_MAXBLK * SEL_LEN and T // SEL_LEN >= N_SEL
    GW = NSA_GROUP * HEAD_DIM
    onehot = (jnp.arange(T)[:, None] // SEL_LEN == jnp.arange(NSA_MAXBLK)[None, :]).astype(BF16)
    qspec = pl.BlockSpec((None, NSA_TQ, GW), lambda b, g, i: (b, i, g))
    cspec = pl.BlockSpec((None, None, CMP_PER_SEL * NSA_MAXBLK, LANES), lambda b, g, i: (b, g, 0, 0))
    kspec = pl.BlockSpec((None, None, T, LANES), lambda b, g, i: (g, b, 0, 0))
    r4 = lambda a: a.reshape(NSA_KV, B, T, LANES)
    out = pl.pallas_call(
        _nsa_body,
        grid=(B, NSA_KV, T // NSA_TQ),
        in_specs=[qspec, qspec,
                  pl.BlockSpec((None, None, NSA_TQ, LANES), lambda b, g, i: (g, b, i, 0)),
                  cspec, cspec, kspec, kspec, kspec, kspec,
                  pl.BlockSpec((T, NSA_MAXBLK), lambda b, g, i: (0, 0))],
        out_specs=qspec,
        out_shape=jax.ShapeDtypeStruct((B, T, NSA_HEADS * HEAD_DIM), F32),
        compiler_params=_cparams("parallel", "parallel", "arbitrary"),
        name="nsa_attention",
    )(qn.reshape(B, T, -1), qr.reshape(B, T, -1), r4(gates), kc, vc, r4(skd), r4(svd), r4(wkd), r4(wvd), onehot)
    return out.reshape(B * T, NSA_HEADS * HEAD_DIM)


def even_mixer_prompt(h, P, e):
    B, T, D = h.shape
    N = B * T
    w_in = jnp.pad(P['w_in_even'][e], ((0, 0), (0, EVEN_PAD - EVEN_COLS)))
    z = matmul(h.reshape(N, D), w_in)
    u = z[:, :LRU_WIDTH].reshape(B, T, LRU_WIDTH)
    y = z[:, LRU_WIDTH:2 * LRU_WIDTH].reshape(B, T, LRU_WIDTH)
    uc, conv_new = causal_conv(u, jnp.zeros((B, CONV_W - 1, LRU_WIDTH), h.dtype), P['conv_w'][e], P['conv_b'][e])
    hl, h_last = rg_lru(uc, jnp.zeros((B, LRU_WIDTH), h.dtype), P['lru_wa'][e], P['lru_ba'][e], P['lru_wi'][e],
                        P['lru_bi'][e], P['lru_lambda'][e])
    lru_out = hl * jax.nn.gelu(y)
    cos, sin = _rope_tables(jnp.tile(jnp.arange(T), B))
    qn, qr, ck, sk, wk, skd, svd, wkd, wvd, gates = nsa_prep(z, cos, sin, P['nsa_q_gain'][e], P['nsa_k_gain'][e])
    base = 2 * LRU_WIDTH + NSA_HEADS * HEAD_DIM
    kvcol = lambda c: z[:, base + c * LANES:base + (c + 1) * LANES]
    cv, sv, wv = kvcol(1), kvcol(3), kvcol(5)
    kc = compress_blocks(ck, P['w_cmp_k'][e], B, T)
    vc = compress_blocks(cv, P['w_cmp_v'][e], B, T)
    nsa_out = nsa_attention(qn, qr, gates, kc, vc, skd, svd, wkd, wvd, B, T)
    mix = matmul(jnp.concatenate([lru_out.reshape(N, LRU_WIDTH), nsa_out], axis=-1), P['w_out_even'][e])
    kvs = (B, T, NSA_KV, HEAD_DIM)
    keep = min(SW_WIN, T)
    r = lambda a: a.reshape(kvs)
    return mix.reshape(B, T, D), (h_last, conv_new, r(ck), r(cv), r(sk), r(sv), r(wk)[:, -keep:], r(wv)[:, -keep:])


def rms_norm(x, g):
    xf = x.astype(jnp.float32)
    y = xf * lax.rsqrt(jnp.mean(xf * xf, axis=-1, keepdims=True) + EPS)
    return (y * g.astype(jnp.float32)).astype(x.dtype)


def rope(x, pos):
    half = HEAD_DIM // 2
    freq = ROPE_THETA ** (-jnp.arange(half, dtype=jnp.float32) / half)
    ang = pos.astype(jnp.float32)[:, None] * freq[None, :]
    cos = jnp.cos(ang)[None, :, None, :]
    sin = jnp.sin(ang)[None, :, None, :]
    xf = x.astype(jnp.float32)
    x1, x2 = xf[..., :half], xf[..., half:]
    return jnp.concatenate([x1 * cos - x2 * sin, x2 * cos + x1 * sin], axis=-1).astype(x.dtype)


def masked_softmax(s, mask):
    s = jnp.where(mask, s.astype(jnp.float32), NEG)
    m = jnp.max(s, axis=-1, keepdims=True)
    e = jnp.where(mask, jnp.exp(s - m), 0.0)
    den = jnp.maximum(jnp.sum(e, axis=-1, keepdims=True), TINY)
    return e / den, m + jnp.log(den)


def split_cols(z, sizes):
    out, o = [], 0
    for s in sizes:
        out.append(z[..., o:o + s])
        o += s
    return out


def over_query_blocks(fn, T):
    if T > Q_BLOCK and T % Q_BLOCK == 0:
        out = lax.map(lambda i: fn(i * Q_BLOCK, Q_BLOCK), jnp.arange(T // Q_BLOCK))
        return jnp.swapaxes(out, 0, 1).reshape(out.shape[1], T, out.shape[-1])
    return fn(0, T)


def gather_pages(pool, page_table):
    g = pool[page_table]
    return g.reshape(g.shape[0], g.shape[1] * g.shape[2], g.shape[3], g.shape[4])


def causal_conv(u, buf, w, b):
    T = u.shape[1]
    up = jnp.concatenate([buf, u], axis=1)
    y = b + up[:, 0:T] * w[0]
    for j in range(1, CONV_W):
        y = y + up[:, j:j + T] * w[j]
    return y, up[:, T:]


def _lin_comb(l, r):
    return (l[0] * r[0], r[0] * l[1] + r[1])


def rg_lru(u, h0, wa, ba, wi, bi, lam):
    B, T, W = u.shape
    ub = u.reshape(B, T, LRU_BLOCKS, LRU_BW)
    r = jax.nn.sigmoid((jnp.einsum('btki,kij->btkj', ub, wa).reshape(B, T, W) + ba).astype(jnp.float32))
    gi = jax.nn.sigmoid((jnp.einsum('btki,kij->btkj', ub, wi).reshape(B, T, W) + bi).astype(jnp.float32))
    log_a = -LRU_C * r * jax.nn.softplus(-lam.astype(jnp.float32))
    a = jnp.exp(log_a)
    b = jnp.sqrt(-jnp.expm1(2.0 * log_a)) * (gi * u.astype(jnp.float32))
    b = b.at[:, 0].add(a[:, 0] * h0.astype(jnp.float32))
    _, h = lax.associative_scan(_lin_comb, (a, b), axis=1)
    return h.astype(u.dtype), h[:, -1].astype(u.dtype)


def compress(rows, w, t_pad):
    B, T = rows.shape[:2]
    r = jnp.pad(rows, ((0, 0), (0, t_pad - T), (0, 0), (0, 0)))
    r = r.reshape(B, t_pad // CMP_LEN, CMP_LEN, NSA_KV, HEAD_DIM)
    return jnp.einsum('bnjgd,jde->bnge', r, w)


def sel_blocks(rows, t_pad):
    B, T = rows.shape[:2]
    r = jnp.pad(rows, ((0, 0), (0, t_pad - T), (0, 0), (0, 0)))
    return r.reshape(B, t_pad // SEL_LEN, SEL_LEN, NSA_KV, HEAD_DIM).transpose(0, 3, 1, 2, 4)


def nsa_attend(q, q_rot, gates, qpos, wstart, kc, vc, ksg, vsg, kw_p, vw_p):
    B, Q = q.shape[:2]
    G, M = NSA_KV, NSA_GROUP
    scale = HEAD_DIM ** -0.5
    qg = q.reshape(B, Q, G, M, HEAD_DIM)
    qr = q_rot.reshape(B, Q, G, M, HEAD_DIM)
    ncb = kc.shape[1]
    s_c = jnp.einsum('bqgmd,bngd->bgmqn', qg, kc) * scale
    cmask = ((jnp.arange(ncb) + 1) * CMP_LEN - 1)[None, :] <= qpos[:, None]
    p_c, _ = masked_softmax(s_c, cmask)
    o_c = jnp.einsum('bgmqn,bngd->bqgmd', p_c.astype(vc.dtype), vc)
    nsb = ksg.shape[2]
    imp = p_c.sum(axis=2).reshape(B, G, Q, nsb, CMP_PER_SEL).sum(-1)
    sb = jnp.arange(nsb)[None, :]
    cur = (qpos // SEL_LEN)[:, None]
    valid = sb * SEL_LEN <= qpos[:, None]
    forced = (sb == 0) | (sb == cur) | (sb == cur - 1)
    score = jnp.where(valid, imp + jnp.where(forced, FORCE, 0.0), NEG)
    top_s, top_i = lax.top_k(score, min(N_SEL, nsb))
    n = top_i.shape[-1]
    bi = jnp.arange(B)[:, None, None, None]
    gi = jnp.arange(G)[None, :, None, None]
    k_sel = ksg[bi, gi, top_i]
    v_sel = vsg[bi, gi, top_i]
    kpos = top_i[..., None] * SEL_LEN + jnp.arange(SEL_LEN)
    smask = (top_s > 0.5 * NEG)[..., None] & (kpos <= qpos[None, None, :, None, None])
    s_s = jnp.einsum('bqgmd,bgqnld->bgmqnl', qr, k_sel).reshape(B, G, M, Q, n * SEL_LEN) * scale
    p_s, _ = masked_softmax(s_s, smask.reshape(B, G, 1, Q, n * SEL_LEN))
    o_s = jnp.einsum('bgmqk,bgqkd->bqgmd', p_s.astype(v_sel.dtype), v_sel.reshape(B, G, Q, n * SEL_LEN, HEAD_DIM))
    span = SW_WIN + Q
    kwb = lax.dynamic_slice_in_dim(kw_p, wstart, span, axis=1)
    vwb = lax.dynamic_slice_in_dim(vw_p, wstart, span, axis=1)
    qloc = wstart + jnp.arange(Q)
    kloc = wstart - SW_WIN + jnp.arange(span)
    dist = qloc[:, None] - kloc[None, :]
    wmask = (dist >= 0) & (dist <= SW_WIN) & (kloc[None, :] >= 0)
    s_w = jnp.einsum('bqgmd,bkgd->bgmqk', qr, kwb) * scale
    p_w, _ = masked_softmax(s_w, wmask)
    o_w = jnp.einsum('bgmqk,bkgd->bqgmd', p_w.astype(vwb.dtype), vwb)
    gt = gates.reshape(B, Q, G, M, 3).astype(o_c.dtype)
    o = gt[..., 0:1] * o_c + gt[..., 1:2] * o_s + gt[..., 2:3] * o_w
    return o.reshape(B, Q, NSA_HEADS * HEAD_DIM)


def even_mixer(h, p0, past, P, e):
    lru_h0, conv_buf, past_ck, past_cv, past_sk, past_sv, win_k, win_v = past
    B, T, _ = h.shape
    z = matmul(h, P['w_in_even'][e])
    u, y, q, ck, cv, sk, sv, wk, wv, g = split_cols(z, EVEN_SIZES)
    uc, conv_new = causal_conv(u, conv_buf, P['conv_w'][e], P['conv_b'][e])
    hl, h_last = rg_lru(uc, lru_h0, P['lru_wa'][e], P['lru_ba'][e], P['lru_wi'][e], P['lru_bi'][e], P['lru_lambda'][e])
    lru_out = hl * jax.nn.gelu(y)
    pos = p0 + jnp.arange(T)
    kg = P['nsa_k_gain'][e]
    kvs = (B, T, NSA_KV, HEAD_DIM)
    q = rms_norm(q.reshape(B, T, NSA_HEADS, HEAD_DIM), P['nsa_q_gain'][e])
    q_rot = rope(q, pos)
    ck = rms_norm(ck.reshape(kvs), kg[0])
    cv = cv.reshape(kvs)
    sk = rope(rms_norm(sk.reshape(kvs), kg[1]), pos)
    sv = sv.reshape(kvs)
    wk = rope(rms_norm(wk.reshape(kvs), kg[2]), pos)
    wv = wv.reshape(kvs)
    gates = jax.nn.sigmoid(g.astype(jnp.float32)).reshape(B, T, NSA_HEADS, 3)
    ck_all = jnp.concatenate([past_ck, ck], axis=1)
    cv_all = jnp.concatenate([past_cv, cv], axis=1)
    sk_all = jnp.concatenate([past_sk, sk], axis=1)
    sv_all = jnp.concatenate([past_sv, sv], axis=1)
    t_k = ck_all.shape[1]
    t_pad = -(-t_k // SEL_LEN) * SEL_LEN
    kc = compress(ck_all, P['w_cmp_k'][e], t_pad)
    vc = compress(cv_all, P['w_cmp_v'][e], t_pad)
    ksg = sel_blocks(sk_all, t_pad)
    vsg = sel_blocks(sv_all, t_pad)
    wk_all = jnp.concatenate([win_k, wk], axis=1)
    wv_all = jnp.concatenate([win_v, wv], axis=1)
    wb = win_k.shape[1]
    pad_w = ((0, 0), (SW_WIN, 0), (0, 0), (0, 0))
    wk_p = jnp.pad(wk_all, pad_w)
    wv_p = jnp.pad(wv_all, pad_w)

    def block(q0, qb):
        sl = lambda a: lax.dynamic_slice_in_dim(a, q0, qb, axis=1)
        qpos = p0 + q0 + jnp.arange(qb)
        return nsa_attend(sl(q), sl(q_rot), sl(gates), qpos, wb + q0, kc, vc, ksg, vsg, wk_p, wv_p)

    nsa_out = over_query_blocks(block, T)
    mix = matmul(jnp.concatenate([lru_out, nsa_out.astype(lru_out.dtype)], axis=-1), P['w_out_even'][e])
    keep = min(SW_WIN, wk_all.shape[1])
    return mix, (h_last, conv_new, ck, cv, sk, sv, wk_all[:, -keep:], wv_all[:, -keep:])


def dilated_attend(q, kp, vp, qloc):
    scale = HEAD_DIM ** -0.5
    outs, lses = [], []
    for w, d in DIL_PATTERNS:
        kl = qloc[:, None] - d * jnp.arange(w // d + 1)[None, :]
        kg = kp[:, kl + C_WINDOW]
        vg = vp[:, kl + C_WINDOW]
        s = jnp.einsum('bqhd,bqnhd->bhqn', q, kg) * scale
        p, lse = masked_softmax(s, kl >= 0)
        outs.append(jnp.einsum('bhqn,bqnhd->bqhd', p.astype(vg.dtype), vg))
        lses.append(lse)
    wts = jax.nn.softmax(jnp.stack(lses), axis=0)
    o = jnp.swapaxes(wts[0], 1, 2).astype(outs[0].dtype) * outs[0]
    for i in range(1, len(DIL_PATTERNS)):
        o = o + jnp.swapaxes(wts[i], 1, 2).astype(outs[i].dtype) * outs[i]
    return o.reshape(o.shape[0], o.shape[1], C_WIDTH)


def odd_mixer(h, p0, past, P, o):
    buf_k, buf_v = past
    B, T, _ = h.shape
    z = matmul(h, P['w_in_odd'][o])
    q, k, v = split_cols(z, (C_WIDTH,) * 3)
    shp = (B, T, C_HEADS, HEAD_DIM)
    pos = p0 + jnp.arange(T)
    q = rope(rms_norm(q.reshape(shp), P['dil_q_gain'][o]), pos)
    k = rope(rms_norm(k.reshape(shp), P['dil_k_gain'][o]), pos)
    v = v.reshape(shp)
    k_all = jnp.concatenate([buf_k, k], axis=1)
    v_all = jnp.concatenate([buf_v, v], axis=1)
    wb = buf_k.shape[1]
    pad = ((0, 0), (C_WINDOW, 0), (0, 0), (0, 0))
    kp = jnp.pad(k_all, pad)
    vp = jnp.pad(v_all, pad)

    def block(q0, qb):
        return dilated_attend(lax.dynamic_slice_in_dim(q, q0, qb, axis=1), kp, vp, wb + q0 + jnp.arange(qb))

    att = over_query_blocks(block, T)
    keep = min(C_WINDOW, k_all.shape[1])
    return matmul(att, P['w_out_odd'][o]), (k_all[:, -keep:], v_all[:, -keep:])


def routed_experts(xf, eid, wts, w_gate, w_up, w_down):
    N, D = xf.shape
    M = N * TOP_K
    e_flat = eid.reshape(M)
    tok = jnp.repeat(jnp.arange(N, dtype=jnp.int32), TOP_K)
    w_flat = wts.reshape(M)
    order = jnp.argsort(e_flat)
    e_sorted = e_flat[order]
    counts = jnp.bincount(e_flat, length=N_EXPERTS)
    padded = (counts + MOE_BLK - 1) // MOE_BLK * MOE_BLK
    start = jnp.cumsum(counts) - counts
    pend = jnp.cumsum(padded)
    pstart = pend - padded
    dest = pstart[e_sorted] + (jnp.arange(M) - start[e_sorted])
    n_blocks = -(-(M + N_EXPERTS * (MOE_BLK - 1)) // MOE_BLK)
    n_slots = n_blocks * MOE_BLK
    slot_tok = jnp.full((n_slots,), N, jnp.int32).at[dest].set(tok[order])
    slot_w = jnp.zeros((n_slots,), xf.dtype).at[dest].set(w_flat[order].astype(xf.dtype))
    blk_e = jnp.minimum(jnp.searchsorted(pend, jnp.arange(n_blocks) * MOE_BLK, side='right'), N_EXPERTS - 1)
    x_pad = jnp.concatenate([xf, jnp.zeros((1, D), xf.dtype)], axis=0)

    def run(args):
        toks, w, e = args
        xb = x_pad[toks]
        hid = jax.nn.silu(xb @ w_gate[e]) * (xb @ w_up[e])
        return (hid @ w_down[e]) * w[:, None]

    yb = lax.map(run, (slot_tok.reshape(n_blocks, MOE_BLK), slot_w.reshape(n_blocks, MOE_BLK), blk_e))
    y = jnp.zeros((N + 1, D), xf.dtype).at[slot_tok].add(yb.reshape(n_slots, D))
    return y[:N]


def hier_moe(x, P, layer):
    B, T, D = x.shape
    N = B * T
    xf = x.reshape(N, D)
    rows = jnp.arange(N)
    g_logit = (xf @ P['w_router_group'][layer] + P['b_router_group'][layer]).astype(jnp.float32)
    g_top = jnp.argmax(g_logit, axis=-1)
    g_w = jax.nn.softmax(g_logit, axis=-1)[rows, g_top]
    e_logit = (xf @ P['w_router_exp'][layer] + P['b_router_exp'][layer]).astype(jnp.float32)
    e_logit = e_logit.reshape(N, N_GROUPS, EXP_PER_GROUP)[rows, g_top]
    top_l, top_i = lax.top_k(e_logit, TOP_K)
    top_p = jax.nn.softmax(top_l, axis=-1)
    eid = (g_top[:, None] * EXP_PER_GROUP + top_i).astype(jnp.int32)
    wts = g_w[:, None] * top_p
    y = routed_experts(xf, eid, wts, P['w_exp_gate'][layer], P['w_exp_up'][layer], P['w_exp_down'][layer])
    return y.reshape(B, T, D)


def trunk(x, p0, even_past, odd_past, P):
    even_new, odd_new = [], []
    for layer in range(DEPTH):
        hn = rms_norm(x, P['norm_mix'][layer])
        if layer % 2 == 0:
            if even_past is None:
                mix, st = even_mixer_prompt(hn, P, layer // 2)
            else:
                mix, st = even_mixer(hn, p0, even_past[layer // 2], P, layer // 2)
            even_new.append(st)
        else:
            if odd_past is None:
                mix, st = odd_mixer_prompt(hn, P, layer // 2)
            else:
                mix, st = odd_mixer(hn, p0, odd_past[layer // 2], P, layer // 2)
            odd_new.append(st)
        x = x + mix.astype(x.dtype)
        x = x + hier_moe(rms_norm(x, P['norm_ffn'][layer]), P, layer).astype(x.dtype)
    return x, even_new, odd_new


def stack_layers(states):
    return [jnp.stack(items) for items in zip(*states)]


def kernel(x_prompt, x_sample, state_lru_h, state_lru_conv, cache_cmp_k, cache_cmp_v, cache_sel_k, cache_sel_v, cache_win_k, cache_win_v, cache_dil_k, cache_dil_v, page_table, norm_mix, norm_ffn, w_in_even, conv_w, conv_b, lru_wa, lru_ba, lru_wi, lru_bi, lru_lambda, nsa_q_gain, nsa_k_gain, w_cmp_k, w_cmp_v, w_out_even, w_in_odd, dil_q_gain, dil_k_gain, w_out_odd, w_router_group, b_router_group, w_router_exp, b_router_exp, w_exp_gate, w_exp_up, w_exp_down):
    P = dict(norm_mix=norm_mix, norm_ffn=norm_ffn, w_in_even=w_in_even, conv_w=conv_w, conv_b=conv_b,
             lru_wa=lru_wa, lru_ba=lru_ba, lru_wi=lru_wi, lru_bi=lru_bi, lru_lambda=lru_lambda,
             nsa_q_gain=nsa_q_gain, nsa_k_gain=nsa_k_gain, w_cmp_k=w_cmp_k, w_cmp_v=w_cmp_v,
             w_out_even=w_out_even, w_in_odd=w_in_odd, dil_q_gain=dil_q_gain, dil_k_gain=dil_k_gain,
             w_out_odd=w_out_odd, w_router_group=w_router_group, b_router_group=b_router_group,
             w_router_exp=w_router_exp, b_router_exp=b_router_exp, w_exp_gate=w_exp_gate,
             w_exp_up=w_exp_up, w_exp_down=w_exp_down)
    y_prompt, ev_p, od_p = trunk(x_prompt, 0, None, None, P)
    p0 = page_table.shape[1] * PAGE_SIZE
    ev1 = [(state_lru_h[e], state_lru_conv[e],
            gather_pages(cache_cmp_k[e], page_table), gather_pages(cache_cmp_v[e], page_table),
            gather_pages(cache_sel_k[e], page_table), gather_pages(cache_sel_v[e], page_table),
            cache_win_k[e], cache_win_v[e]) for e in range(N_EVEN)]
    od1 = [(cache_dil_k[o], cache_dil_v[o]) for o in range(N_ODD)]
    y_sample, ev_s, od_s = trunk(x_sample, p0, ev1, od1, P)
    p_lru_h, p_lru_conv, p_cmp_k, p_cmp_v, p_sel_k, p_sel_v, p_win_k, p_win_v = stack_layers(ev_p)
    p_dil_k, p_dil_v = stack_layers(od_p)
    s_lru_h, s_lru_conv, s_cmp_k, s_cmp_v, s_sel_k, s_sel_v, s_win_k, s_win_v = stack_layers(ev_s)
    s_dil_k, s_dil_v = stack_layers(od_s)
    return (y_prompt, y_sample, p_lru_h, p_lru_conv, p_cmp_k, p_cmp_v, p_sel_k, p_sel_v, p_win_k, p_win_v, p_dil_k, p_dil_v, s_lru_h, s_lru_conv, s_cmp_k, s_cmp_v, s_sel_k, s_sel_v, s_win_k, s_win_v, s_dil_k, s_dil_v)
```

```python
import functools

import jax, jax.numpy as jnp
from jax import lax
import numpy as np
from jax.experimental import pallas as pl
from jax.experimental.pallas import tpu as pltpu

D_MODEL = 1024
BATCH = 2
SEQ = 8192
DEPTH = 2
DEC_BATCH = 128
DEC_SEQ = 4
PAST_LEN = 2048
PAGE_SIZE = 128

HEAD_DIM = 64
LRU_WIDTH = D_MODEL // 2
LRU_BLOCKS = LRU_WIDTH // HEAD_DIM
LRU_BW = LRU_WIDTH // LRU_BLOCKS
CONV_W = 4
LRU_C = 8.0
NSA_HEADS = (D_MODEL // 2) // HEAD_DIM
NSA_KV = 2
NSA_GROUP = NSA_HEADS // NSA_KV
CMP_LEN = 32
SEL_LEN = 64
CMP_PER_SEL = SEL_LEN // CMP_LEN
N_SEL = 16
SW_WIN = 512
FORCE = 1000.0
C_HEADS = D_MODEL // HEAD_DIM
DIL_PATTERNS = ((128, 1), (512, 4), (2048, 16))
C_WINDOW = 2048
N_GROUPS = 4
EXP_PER_GROUP = 8
N_EXPERTS = N_GROUPS * EXP_PER_GROUP
D_EXPERT = D_MODEL // 2
TOP_K = 2
MOE_BLK = 128
Q_BLOCK = 128
ROPE_THETA = 10000.0
EPS = 1e-6
NEG = -1e30
TINY = 1e-30
N_EVEN = (DEPTH + 1) // 2
N_ODD = DEPTH // 2
EVEN_SIZES = (LRU_WIDTH, LRU_WIDTH, NSA_HEADS * HEAD_DIM) + (NSA_KV * HEAD_DIM,) * 6 + (3 * NSA_HEADS,)
EVEN_COLS = sum(EVEN_SIZES)
EVEN_OUT = LRU_WIDTH + NSA_HEADS * HEAD_DIM
C_WIDTH = C_HEADS * HEAD_DIM


def _mm_body(x_ref, w_ref, o_ref):
    o_ref[...] = jnp.dot(x_ref[...].astype(jnp.bfloat16), w_ref[...].astype(jnp.bfloat16),
                         preferred_element_type=jnp.float32)


def matmul(x, w, tm=512, tn=512):
    lead = x.shape[:-1]
    K = x.shape[-1]
    M = w.shape[-1]
    x2 = x.reshape(-1, K)
    N = x2.shape[0]
    tm = min(tm, N)
    tn = min(tn, M)
    out = pl.pallas_call(
        _mm_body,
        grid=(pl.cdiv(N, tm), pl.cdiv(M, tn)),
        in_specs=[pl.BlockSpec((tm, K), lambda i, j: (i, 0)),
                  pl.BlockSpec((K, tn), lambda i, j: (0, j))],
        out_specs=pl.BlockSpec((tm, tn), lambda i, j: (i, j)),
        out_shape=jax.ShapeDtypeStruct((N, M), jnp.float32),
        name="matmul",
    )(x2, w)
    return out.reshape(lead + (M,))


LANES = 128
VMEM_LIMIT = 48 * 1024 * 1024
BF16 = jnp.bfloat16
F32 = jnp.float32


def _cparams(*sem):
    return pltpu.CompilerParams(dimension_semantics=sem, vmem_limit_bytes=VMEM_LIMIT)


def _rope_tables(pos):
    half = HEAD_DIM // 2
    freq = ROPE_THETA ** (-jnp.arange(half, dtype=F32) / half)
    ang = pos.astype(F32)[:, None] * freq[None, :]
    cos, sin = jnp.cos(ang), jnp.sin(ang)
    return jnp.tile(cos, (1, 4)), jnp.tile(jnp.concatenate([-sin, sin], axis=1), (1, 2))


def _head_mean_matrix(width):
    h = jnp.arange(width) // HEAD_DIM
    return jnp.where(h[:, None] == h[None, :], 1.0 / HEAD_DIM, 0.0).astype(BF16)


def _head_norm(x, gain, gmat):
    x2 = x * x
    hi = x2.astype(BF16)
    lo = (x2 - hi.astype(F32)).astype(BF16)
    ms = jnp.dot(hi, gmat, preferred_element_type=F32) + jnp.dot(lo, gmat, preferred_element_type=F32)
    return x * lax.rsqrt(ms + EPS) * gain


def _rope_lanes(y, cos, sin):
    hi_half = (lax.broadcasted_iota(jnp.int32, (y.shape[0], LANES), 1) & (HEAD_DIM // 2)) != 0
    outs = []
    for c in range(y.shape[1] // LANES):
        yc = y[:, c * LANES:(c + 1) * LANES]
        partner = jnp.where(hi_half, pltpu.roll(yc, HEAD_DIM // 2, 1), pltpu.roll(yc, LANES - HEAD_DIM // 2, 1))
        outs.append(yc * cos + partner * sin)
    return outs[0] if len(outs) == 1 else jnp.concatenate(outs, axis=1)


def _inproj_odd_body(x_ref, w_ref, cos_ref, sin_ref, gain_ref, gmat_ref, o_ref, *, n_rope):
    j = pl.program_id(1)
    acc = jnp.dot(x_ref[...].astype(BF16), w_ref[...].astype(BF16), preferred_element_type=F32)

    @pl.when(j < n_rope)
    def _():
        o_ref[...] = _rope_lanes(_head_norm(acc, gain_ref[0], gmat_ref[...]), cos_ref[...], sin_ref[...])

    @pl.when(j >= n_rope)
    def _():
        o_ref[...] = acc


def inproj_odd(h2, w, q_gain, k_gain, cos, sin, tm=512, tn=512):
    N, K = h2.shape
    M = w.shape[-1]
    tm = min(tm, N)
    reps = tn // HEAD_DIM
    per = C_WIDTH // tn
    gains = jnp.concatenate([jnp.tile(jnp.tile(q_gain, reps)[None], (per, 1)),
                             jnp.tile(jnp.tile(k_gain, reps)[None], (per, 1)),
                             jnp.ones((per, tn), F32)], axis=0)[:, None, :]
    return pl.pallas_call(
        functools.partial(_inproj_odd_body, n_rope=2 * per),
        grid=(N // tm, M // tn),
        in_specs=[pl.BlockSpec((tm, K), lambda i, j: (i, 0)),
                  pl.BlockSpec((K, tn), lambda i, j: (0, j)),
                  pl.BlockSpec((tm, LANES), lambda i, j: (i, 0)),
                  pl.BlockSpec((tm, LANES), lambda i, j: (i, 0)),
                  pl.BlockSpec((1, 1, tn), lambda i, j: (j, 0, 0)),
                  pl.BlockSpec((tn, tn), lambda i, j: (0, 0))],
        out_specs=pl.BlockSpec((tm, tn), lambda i, j: (i, j)),
        out_shape=jax.ShapeDtypeStruct((N, M), F32),
        compiler_params=_cparams("parallel", "arbitrary"),
        name="inproj_odd",
    )(h2, w, cos, sin, gains, _head_mean_matrix(tn))


DIL_TQ = 128


def _dil_body(q_ref, kc_ref, kp_ref, vc_ref, vp_ref, o_ref, lse_ref):
    i = pl.program_id(2)
    tq = q_ref.shape[0]
    row = lax.broadcasted_iota(jnp.int32, (tq, tq), 0)
    col = lax.broadcasted_iota(jnp.int32, (tq, tq), 1)
    mask_cur = col <= row
    mask_prev = jnp.logical_and(col >= row, i > 0)
    low = lax.broadcasted_iota(jnp.int32, (tq, LANES), 1) < HEAD_DIM
    scale = HEAD_DIM ** -0.5
    dn = (((1,), (1,)), ((), ()))
    for hp in range(q_ref.shape[1] // LANES):
        sl = slice(hp * LANES, (hp + 1) * LANES)
        q2 = q_ref[:, sl] * scale
        kc = kc_ref[:, sl].astype(BF16)
        kp = kp_ref[:, sl].astype(BF16)
        vc = vc_ref[:, sl].astype(BF16)
        vp = vp_ref[:, sl].astype(BF16)
        outs, lses = [], []
        for half in range(2):
            qh = jnp.where(low if half == 0 else jnp.logical_not(low), q2, 0.0).astype(BF16)
            s_c = jnp.where(mask_cur, lax.dot_general(qh, kc, dn, preferred_element_type=F32), NEG)
            s_p = jnp.where(mask_prev, lax.dot_general(qh, kp, dn, preferred_element_type=F32), NEG)
            m = jnp.maximum(jnp.max(s_c, axis=-1, keepdims=True), jnp.max(s_p, axis=-1, keepdims=True))
            e_c = jnp.exp(s_c - m)
            e_p = jnp.exp(s_p - m)
            den = jnp.maximum(jnp.sum(e_c, axis=-1, keepdims=True) + jnp.sum(e_p, axis=-1, keepdims=True), TINY)
            o = (jnp.dot(e_c.astype(BF16), vc, preferred_element_type=F32)
                 + jnp.dot(e_p.astype(BF16), vp, preferred_element_type=F32))
            outs.append(o / den)
            lses.append(jnp.broadcast_to(m + jnp.log(den), (tq, LANES)))
        o_ref[:, sl] = jnp.where(low, outs[0], outs[1])
        lse_ref[:, sl] = jnp.where(low, lses[0], lses[1])


def dilated_pattern(z, B, T, d):
    W = C_WIDTH
    Td = T // d
    zv = z.reshape(B, Td, d * 3 * W)
    blk = lambda off, prev: pl.BlockSpec(
        (None, DIL_TQ, W), (lambda b, r, i: (b, jnp.maximum(i - 1, 0), 3 * r + off)) if prev
        else (lambda b, r, i: (b, i, 3 * r + off)))
    o_spec = pl.BlockSpec((None, DIL_TQ, W), lambda b, r, i: (b, i, r))
    out, lse = pl.pallas_call(
        _dil_body,
        grid=(B, d, Td // DIL_TQ),
        in_specs=[blk(0, False), blk(1, False), blk(1, True), blk(2, False), blk(2, True)],
        out_specs=[o_spec, o_spec],
        out_shape=[jax.ShapeDtypeStruct((B, Td, d * W), F32)] * 2,
        compiler_params=_cparams("parallel", "parallel", "arbitrary"),
        name="dilated_d%d" % d,
    )(zv, zv, zv, zv, zv)
    return out.reshape(B * T, W), lse.reshape(B * T, W)


def _combine_outproj_body(o1, o2, o3, l1, l2, l3, w_ref, out_ref):
    m = jnp.maximum(jnp.maximum(l1[...], l2[...]), l3[...])
    e1, e2, e3 = jnp.exp(l1[...] - m), jnp.exp(l2[...] - m), jnp.exp(l3[...] - m)
    den = e1 + e2 + e3
    att = (e1 / den) * o1[...] + (e2 / den) * o2[...] + (e3 / den) * o3[...]
    out_ref[...] = jnp.dot(att.astype(BF16), w_ref[...].astype(BF16), preferred_element_type=F32)


def combine_outproj(outs, lses, w, tm=256):
    N, W = outs[0].shape
    M = w.shape[-1]
    row = pl.BlockSpec((tm, W), lambda i: (i, 0))
    return pl.pallas_call(
        _combine_outproj_body,
        grid=(N // tm,),
        in_specs=[row] * 6 + [pl.BlockSpec((W, M), lambda i: (0, 0))],
        out_specs=pl.BlockSpec((tm, M), lambda i: (i, 0)),
        out_shape=jax.ShapeDtypeStruct((N, M), F32),
        compiler_params=_cparams("parallel"),
        name="dil_combine_outproj",
    )(*outs, *lses, w)


def odd_mixer_prompt(h, P, o):
    B, T, D = h.shape
    assert all(w // d == DIL_TQ and T % (d * DIL_TQ) == 0 for w, d in DIL_PATTERNS)
    cos, sin = _rope_tables(jnp.tile(jnp.arange(T), B))
    z = inproj_odd(h.reshape(B * T, D), P['w_in_odd'][o], P['dil_q_gain'][o], P['dil_k_gain'][o], cos, sin)
    res = [dilated_pattern(z, B, T, d) for _, d in DIL_PATTERNS]
    mix = combine_outproj([r[0] for r in res], [r[1] for r in res], P['w_out_odd'][o])
    keep = min(C_WINDOW, T)
    shp = (B, T, C_HEADS, HEAD_DIM)
    k = z[:, C_WIDTH:2 * C_WIDTH].reshape(shp)
    v = z[:, 2 * C_WIDTH:].reshape(shp)
    return mix.reshape(B, T, D), (k[:, -keep:], v[:, -keep:])


EVEN_PAD = -(-EVEN_COLS // LANES) * LANES
NSA_TQ = 256
NSA_TK = 512
NSA_SPAN = SW_WIN + NSA_TQ
NSA_MAXBLK = LANES
SEL_OFF = -1e9
M_INIT = -3e38


def _dup_group(x, low):
    r = pltpu.roll(x, HEAD_DIM, 1)
    return jnp.where(low, x, r), jnp.where(low, r, x)


def _nsa_prep_body(zq_ref, zkv_ref, zg_ref, cos_ref, sin_ref, qg_ref, kg_ref, gmat_ref,
                   qn_ref, qr_ref, ck_ref, sk_ref, wk_ref, skd_ref, svd_ref, wkd_ref, wvd_ref, gt_ref):
    cos, sin = cos_ref[...], sin_ref[...]
    low = lax.broadcasted_iota(jnp.int32, (zq_ref.shape[0], LANES), 1) < HEAD_DIM
    qn = _head_norm(zq_ref[...], qg_ref[...], gmat_ref[...])
    qn_ref[...] = qn
    qr_ref[...] = _rope_lanes(qn, cos, sin)
    g1 = gmat_ref[0:LANES, 0:LANES]
    col = lambda c: zkv_ref[:, c * LANES:(c + 1) * LANES]
    ck_ref[...] = _head_norm(col(0), kg_ref[0:1, :], g1)
    sk = _rope_lanes(_head_norm(col(2), kg_ref[1:2, :], g1), cos, sin)
    wk = _rope_lanes(_head_norm(col(4), kg_ref[2:3, :], g1), cos, sin)
    sk_ref[...] = sk
    wk_ref[...] = wk
    for x, ref in ((sk, skd_ref), (col(3), svd_ref), (wk, wkd_ref), (col(5), wvd_ref)):
        d0, d1 = _dup_group(x, low)
        ref[0] = d0.astype(BF16)
        ref[1] = d1.astype(BF16)
    gt = 1.0 / (1.0 + jnp.exp(-zg_ref[...]))
    gt_ref[0] = gt
    gt_ref[1] = pltpu.roll(gt, LANES - 3 * NSA_GROUP, 1)


def nsa_prep(z, cos, sin, q_gain, k_gain, tm=256):
    N = z.shape[0]
    tm = min(tm, N)
    QW = NSA_HEADS * HEAD_DIM
    KW = NSA_KV * HEAD_DIM
    assert KW == LANES and QW % LANES == 0 and (LRU_WIDTH * 2) % QW == 0
    f = lambda shape: jax.ShapeDtypeStruct(shape, F32)
    b = lambda shape: jax.ShapeDtypeStruct(shape, BF16)
    row = lambda w: pl.BlockSpec((tm, w), lambda i: (i, 0))
    grp = pl.BlockSpec((NSA_KV, tm, LANES), lambda i: (0, i, 0))
    return pl.pallas_call(
        _nsa_prep_body,
        grid=(N // tm,),
        in_specs=[pl.BlockSpec((tm, QW), lambda i: (i, 2 * LRU_WIDTH // QW)),
                  pl.BlockSpec((tm, 6 * KW), lambda i: (i, (2 * LRU_WIDTH + QW) // (6 * KW))),
                  pl.BlockSpec((tm, LANES), lambda i: (i, (2 * LRU_WIDTH + QW + 6 * KW) // LANES)),
                  row(LANES), row(LANES),
                  pl.BlockSpec((1, QW), lambda i: (0, 0)),
                  pl.BlockSpec((3, LANES), lambda i: (0, 0)),
                  pl.BlockSpec((QW, QW), lambda i: (0, 0))],
        out_specs=[row(QW), row(QW), row(LANES), row(LANES), row(LANES), grp, grp, grp, grp, grp],
        out_shape=[f((N, QW)), f((N, QW)), f((N, LANES)), f((N, LANES)), f((N, LANES)),
                   b((NSA_KV, N, LANES)), b((NSA_KV, N, LANES)), b((NSA_KV, N, LANES)), b((NSA_KV, N, LANES)),
                   f((NSA_KV, N, LANES))],
        compiler_params=_cparams("parallel"),
        name="nsa_prep",
    )(z, z, z, cos, sin, jnp.tile(q_gain, NSA_HEADS)[None], jnp.tile(k_gain, (1, NSA_KV)), _head_mean_matrix(QW))


def compress_blocks(rows, w, B, T):
    nsb = T // SEL_LEN
    wc = jnp.einsum('jde,gh->jgdhe', w, jnp.eye(NSA_KV, dtype=w.dtype)).reshape(CMP_LEN * LANES, LANES)
    kc = matmul(rows.reshape(B * T // CMP_LEN, CMP_LEN * LANES), wc, tm=256)
    kc = kc.reshape(B, nsb, CMP_PER_SEL, NSA_KV, HEAD_DIM).transpose(0, 3, 2, 1, 4)
    kc = jnp.pad(kc, ((0, 0), (0, 0), (0, 0), (0, NSA_MAXBLK - nsb), (0, 0)))
    kc = kc.reshape(B, NSA_KV, CMP_PER_SEL * NSA_MAXBLK, HEAD_DIM)
    return jnp.concatenate([kc, kc], axis=-1)


def _flash_step(s, v, m_, l_, acc):
    m_new = jnp.maximum(m_, jnp.max(s, axis=-1, keepdims=True))
    alpha = jnp.exp(m_ - m_new)
    p = jnp.exp(s - m_new)
    l_new = alpha * l_ + jnp.sum(p, axis=-1, keepdims=True)
    return m_new, l_new, alpha * acc + jnp.dot(p.astype(BF16), v, preferred_element_type=F32)


def _nsa_body(qn_ref, qr_ref, gate_ref, kc_ref, vc_ref, sk_ref, sv_ref, wk_ref, wv_ref, e_ref, o_ref):
    i = pl.program_id(2)
    tq = NSA_TQ
    R = NSA_GROUP * tq
    q0 = i * tq
    scale = HEAD_DIM ** -0.5
    dn = (((1,), (1,)), ((), ()))
    lane = lax.broadcasted_iota(jnp.int32, (tq, LANES), 1)
    low = lane < HEAD_DIM
    qpos1 = q0 + lax.broadcasted_iota(jnp.int32, (tq, 1), 0)
    qpos = q0 + (lax.broadcasted_iota(jnp.int32, (R, 1), 0) & (tq - 1))
    unstack = lambda x: [x[m * tq:(m + 1) * tq] for m in range(NSA_GROUP)]

    def stack(ref):
        parts = []
        for m in range(NSA_GROUP):
            pair, half = divmod(m, 2)
            x = ref[:, pair * LANES:(pair + 1) * LANES] * scale
            parts.append(jnp.where(low if half == 0 else jnp.logical_not(low), x, 0.0))
        return jnp.concatenate(parts, axis=0).astype(BF16)

    cidx = lax.broadcasted_iota(jnp.int32, (R, CMP_PER_SEL * NSA_MAXBLK), 1)
    par = (cidx >= NSA_MAXBLK).astype(jnp.int32)
    blk = CMP_PER_SEL * (cidx - NSA_MAXBLK * par) + par
    cmask = (blk + 1) * CMP_LEN - 1 <= qpos
    s = jnp.where(cmask, lax.dot_general(stack(qn_ref), kc_ref[...].astype(BF16), dn, preferred_element_type=F32), NEG)
    mx = jnp.max(s, axis=-1, keepdims=True)
    e = jnp.where(cmask, jnp.exp(s - mx), 0.0)
    p = e / jnp.maximum(jnp.sum(e, axis=-1, keepdims=True), TINY)
    o_c = jnp.dot(p.astype(BF16), vc_ref[...].astype(BF16), preferred_element_type=F32)
    heads = unstack(p)
    psum = heads[0]
    for x in heads[1:]:
        psum = psum + x
    imp = psum[:, :NSA_MAXBLK] + psum[:, NSA_MAXBLK:]

    cur = lax.shift_right_logical(qpos1, SEL_LEN.bit_length() - 1)
    valid = lane * SEL_LEN <= qpos1
    forced = (lane == 0) | (lane == cur) | (lane == cur - 1)
    score = jnp.where(valid, imp + jnp.where(forced, FORCE, 0.0), NEG)
    lane_f = lane.astype(F32)

    def pick(_, carry):
        sc, sel = carry
        mx = jnp.max(sc, axis=-1, keepdims=True)
        first = jnp.min(jnp.where(sc == mx, lane_f, float(LANES)), axis=-1, keepdims=True)
        hit = lane_f == first
        sel = jnp.where(jnp.logical_and(hit, mx > 0.5 * NEG), 0.0, sel)
        return jnp.where(hit, M_INIT, sc), sel

    _, selneg = lax.fori_loop(0, N_SEL, pick, (score, jnp.full((tq, LANES), SEL_OFF, F32)), unroll=True)
    qr = stack(qr_ref)
    q_aug = jnp.concatenate([qr, jnp.concatenate([selneg.astype(BF16)] * NSA_GROUP, axis=0)], axis=1)

    def sweep(t, carry):
        ks = pl.multiple_of(t * NSA_TK, NSA_TK)
        k_aug = jnp.concatenate([sk_ref[pl.ds(ks, NSA_TK), :], e_ref[pl.ds(ks, NSA_TK), :]], axis=1)
        s = lax.dot_general(q_aug, k_aug, dn, preferred_element_type=F32)
        return _flash_step(s, sv_ref[pl.ds(ks, NSA_TK), :], *carry)

    n_full = q0 // NSA_TK
    init = (jnp.full((R, 1), M_INIT, F32), jnp.zeros((R, 1), F32), jnp.zeros((R, LANES), F32))
    carry = lax.fori_loop(0, n_full, sweep, init)
    ws = pl.multiple_of(jnp.maximum(q0 - SW_WIN, 0), NSA_TQ)
    kpos = ws + lax.broadcasted_iota(jnp.int32, (1, NSA_SPAN), 1)
    dist = qpos - kpos
    k_aug = jnp.concatenate([sk_ref[pl.ds(ws, NSA_SPAN), :], e_ref[pl.ds(ws, NSA_SPAN), :]], axis=1)
    s = jnp.where(jnp.logical_and(kpos >= n_full * NSA_TK, dist >= 0),
                  lax.dot_general(q_aug, k_aug, dn, preferred_element_type=F32), NEG)
    _, l_s, acc_s = _flash_step(s, sv_ref[pl.ds(ws, NSA_SPAN), :], *carry)
    o_s = acc_s / l_s

    s = jnp.where(jnp.logical_and(dist >= 0, dist <= SW_WIN),
                  lax.dot_general(qr, wk_ref[pl.ds(ws, NSA_SPAN), :], dn, preferred_element_type=F32), NEG)
    mx = jnp.max(s, axis=-1, keepdims=True)
    e = jnp.exp(s - mx)
    den = jnp.maximum(jnp.sum(e, axis=-1, keepdims=True), TINY)
    o_w = jnp.dot(e.astype(BF16), wv_ref[pl.ds(ws, NSA_SPAN), :], preferred_element_type=F32) / den

    gates = gate_ref[...]
    gate = lambda j: jnp.concatenate([gates[:, 3 * m + j:3 * m + j + 1] for m in range(NSA_GROUP)], axis=0)
    outs = unstack(gate(0) * o_c + gate(1) * o_s + gate(2) * o_w)
    for pair in range(NSA_GROUP // 2):
        o_ref[:, pair * LANES:(pair + 1) * LANES] = jnp.where(low, outs[2 * pair], outs[2 * pair + 1])


def nsa_attention(qn, qr, gates, kc, vc, skd, svd, wkd, wvd, B, T):
    assert T % NSA_TK == 0 and NSA_SPAN <= T <= NSA_MAXBLK * SEL_LEN and T // SEL_LEN >= N_SEL
    GW = NSA_GROUP * HEAD_DIM
    onehot = (jnp.arange(T)[:, None] // SEL_LEN == jnp.arange(NSA_MAXBLK)[None, :]).astype(BF16)
    qspec = pl.BlockSpec((None, NSA_TQ, GW), lambda b, g, i: (b, i, g))
    cspec = pl.BlockSpec((None, None, CMP_PER_SEL * NSA_MAXBLK, LANES), lambda b, g, i: (b, g, 0, 0))
    kspec = pl.BlockSpec((None, None, T, LANES), lambda b, g, i: (g, b, 0, 0))
    r4 = lambda a: a.reshape(NSA_KV, B, T, LANES)
    out = pl.pallas_call(
        _nsa_body,
        grid=(B, NSA_KV, T // NSA_TQ),
        in_specs=[qspec, qspec,
                  pl.BlockSpec((None, None, NSA_TQ, LANES), lambda b, g, i: (g, b, i, 0)),
                  cspec, cspec, kspec, kspec, kspec, kspec,
                  pl.BlockSpec((T, NSA_MAXBLK), lambda b, g, i: (0, 0))],
        out_specs=qspec,
        out_shape=jax.ShapeDtypeStruct((B, T, NSA_HEADS * HEAD_DIM), F32),
        compiler_params=_cparams("parallel", "parallel", "arbitrary"),
        name="nsa_attention",
    )(qn.reshape(B, T, -1), qr.reshape(B, T, -1), r4(gates), kc, vc, r4(skd), r4(svd), r4(wkd), r4(wvd), onehot)
    return out.reshape(B * T, NSA_HEADS * HEAD_DIM)


LRU_TM = 256
SUBLANES = 8


def _lru_body(u_ref, y_ref, cw_ref, cb_ref, wa_ref, ba_ref, wi_ref, bi_ref, sp_ref, h0_ref, cbuf_ref,
              o_ref, hlast_ref, ctail_ref, h_sc, tail_sc, a_sc, b_sc):
    tm = u_ref.shape[0]

    @pl.when(pl.program_id(1) == 0)
    def _():
        h_sc[...] = h0_ref[...]
        tail_sc[...] = cbuf_ref[...]

    u = u_ref[...]
    ext = jnp.concatenate([tail_sc[...], u], axis=0)
    uc = cb_ref[...]
    for j in range(CONV_W):
        lo = SUBLANES - (CONV_W - 1) + j
        uc = uc + ext[lo:lo + tm] * cw_ref[j:j + 1, :]
    tail_sc[...] = u[tm - SUBLANES:tm]
    ctail_ref[...] = u[tm - SUBLANES:tm]
    ub = uc.astype(BF16)
    r = jax.nn.sigmoid(jnp.dot(ub, wa_ref[...].astype(BF16), preferred_element_type=F32) + ba_ref[...])
    gi = jax.nn.sigmoid(jnp.dot(ub, wi_ref[...].astype(BF16), preferred_element_type=F32) + bi_ref[...])
    log_a = -LRU_C * r * sp_ref[...]
    a = jnp.exp(log_a)
    a_sc[...] = a
    b_sc[...] = jnp.sqrt(-jnp.tanh(log_a) * (a * a + 1.0)) * (gi * uc)

    def step(k, h):
        r0 = pl.multiple_of(k * SUBLANES, SUBLANES)
        a8 = a_sc[pl.ds(r0, SUBLANES), :]
        b8 = b_sc[pl.ds(r0, SUBLANES), :]
        rows = []
        for j in range(SUBLANES):
            h = a8[j:j + 1] * h + b8[j:j + 1]
            rows.append(h)
        b_sc[pl.ds(r0, SUBLANES), :] = jnp.concatenate(rows, axis=0)
        return h

    h = lax.fori_loop(0, tm // SUBLANES, step, h_sc[...])
    h_sc[...] = h
    hlast_ref[...] = h
    o_ref[...] = b_sc[...] * jax.nn.gelu(y_ref[...])


def rg_lru_prompt(z, B, T, h0, conv_buf, P, e):
    W = LRU_WIDTH
    tm = LRU_TM
    assert T % tm == 0 and tm >= SUBLANES and CONV_W - 1 <= SUBLANES
    nt = T // tm
    bd = lambda w: jnp.einsum('kij,kl->kilj', w, jnp.eye(LRU_BLOCKS, dtype=w.dtype)).reshape(W, W)
    vec = lambda v: v.reshape(1, W)
    cbuf = jnp.pad(conv_buf, ((0, 0), (SUBLANES - (CONV_W - 1), 0), (0, 0)))
    const = lambda shape: pl.BlockSpec(shape, lambda b, i: (0,) * len(shape))
    per_b = lambda rows: pl.BlockSpec((None, rows, W), lambda b, i: (b, 0, 0))
    out, h_last, ctail = pl.pallas_call(
        _lru_body,
        grid=(B, nt),
        in_specs=[pl.BlockSpec((tm, W), lambda b, i: (b * nt + i, 0)),
                  pl.BlockSpec((tm, W), lambda b, i: (b * nt + i, 1)),
                  const((CONV_W, W)), const((1, W)), const((W, W)), const((1, W)), const((W, W)), const((1, W)),
                  const((1, W)), per_b(1), per_b(SUBLANES)],
        out_specs=[pl.BlockSpec((tm, W), lambda b, i: (b * nt + i, 0)), per_b(1), per_b(SUBLANES)],
        out_shape=[jax.ShapeDtypeStruct((B * T, W), F32), jax.ShapeDtypeStruct((B, 1, W), F32),
                   jax.ShapeDtypeStruct((B, SUBLANES, W), F32)],
        scratch_shapes=[pltpu.VMEM((1, W), F32), pltpu.VMEM((SUBLANES, W), F32),
                        pltpu.VMEM((tm, W), F32), pltpu.VMEM((tm, W), F32)],
        compiler_params=_cparams("parallel", "arbitrary"),
        name="rg_lru",
    )(z, z, P['conv_w'][e], vec(P['conv_b'][e]), bd(P['lru_wa'][e]), vec(P['lru_ba'][e]), bd(P['lru_wi'][e]),
      vec(P['lru_bi'][e]), vec(jax.nn.softplus(-P['lru_lambda'][e].astype(F32))), h0[:, None, :], cbuf)
    return out, h_last[:, 0], ctail[:, SUBLANES - (CONV_W - 1):]


def even_mixer_prompt(h, P, e):
    B, T, D = h.shape
    N = B * T
    w_in = jnp.pad(P['w_in_even'][e], ((0, 0), (0, EVEN_PAD - EVEN_COLS)))
    z = matmul(h.reshape(N, D), w_in)
    lru_out, h_last, conv_new = rg_lru_prompt(z, B, T, jnp.zeros((B, LRU_WIDTH), F32),
                                              jnp.zeros((B, CONV_W - 1, LRU_WIDTH), F32), P, e)
    cos, sin = _rope_tables(jnp.tile(jnp.arange(T), B))
    qn, qr, ck, sk, wk, skd, svd, wkd, wvd, gates = nsa_prep(z, cos, sin, P['nsa_q_gain'][e], P['nsa_k_gain'][e])
    base = 2 * LRU_WIDTH + NSA_HEADS * HEAD_DIM
    kvcol = lambda c: z[:, base + c * LANES:base + (c + 1) * LANES]
    cv, sv, wv = kvcol(1), kvcol(3), kvcol(5)
    kc = compress_blocks(ck, P['w_cmp_k'][e], B, T)
    vc = compress_blocks(cv, P['w_cmp_v'][e], B, T)
    nsa_out = nsa_attention(qn, qr, gates, kc, vc, skd, svd, wkd, wvd, B, T)
    mix = matmul(jnp.concatenate([lru_out, nsa_out], axis=-1), P['w_out_even'][e])
    kvs = (B, T, NSA_KV, HEAD_DIM)
    keep = min(SW_WIN, T)
    r = lambda a: a.reshape(kvs)
    return mix.reshape(B, T, D), (h_last, conv_new, r(ck), r(cv), r(sk), r(sv), r(wk)[:, -keep:], r(wv)[:, -keep:])


def _bf16_round(x):
    return x.astype(BF16).astype(F32)


def _split_dot(x, m):
    hi = x.astype(BF16)
    lo = (x - hi.astype(F32)).astype(BF16)
    return jnp.dot(hi, m, preferred_element_type=F32) + jnp.dot(lo, m, preferred_element_type=F32)


def _dil_decode_body(q_ref, kn_ref, vn_ref, k1_ref, k4_ref, k16_ref, v1_ref, v4_ref, v16_ref, hsum_ref, hexp_ref, o_ref):
    nq = o_ref.shape[0]
    W = C_WIDTH
    scale = HEAD_DIM ** -0.5
    hsum = hsum_ref[...]
    hexp = hexp_ref[...]
    crow = lax.broadcasted_iota(jnp.int32, (DIL_TQ, 1), 0)
    nrow = lax.broadcasted_iota(jnp.int32, (q_ref.shape[0], 1), 0)
    kn = _bf16_round(kn_ref[...])
    vn = _bf16_round(vn_ref[...])
    k1 = _bf16_round(k1_ref[...])
    v1 = _bf16_round(v1_ref[...])
    outs = []
    for t in range(nq):
        qt = _bf16_round(q_ref[t:t + 1, :] * scale)
        cs = slice(t * W, (t + 1) * W)
        keys = (k1, _bf16_round(k4_ref[:, cs]), _bf16_round(k16_ref[:, cs]))
        vals = (v1, _bf16_round(v4_ref[:, cs]), _bf16_round(v16_ref[:, cs]))
        s = [_split_dot(k * qt, hsum) for k in keys]
        s[0] = jnp.where(crow >= t, s[0], NEG)
        s_new = _split_dot(kn * qt, hsum)
        mult = (nrow <= t).astype(F32) + (len(DIL_PATTERNS) - 1) * (nrow == t).astype(F32)
        s_new = jnp.where(nrow <= t, s_new, NEG)
        m = jnp.max(s_new, axis=0, keepdims=True)
        for x in s:
            m = jnp.maximum(m, jnp.max(x, axis=0, keepdims=True))
        e = [jnp.exp(x - m) for x in s]
        e_new = mult * jnp.exp(s_new - m)
        den = jnp.sum(e_new, axis=0, keepdims=True)
        for x in e:
            den = den + jnp.sum(x, axis=0, keepdims=True)
        den = jnp.maximum(den, TINY)
        acc = jnp.sum(jnp.dot((e_new / den).astype(BF16), hexp, preferred_element_type=F32) * vn, axis=0, keepdims=True)
        for x, v in zip(e, vals):
            p = jnp.dot((x / den).astype(BF16), hexp, preferred_element_type=F32)
            acc = acc + jnp.sum(p * v, axis=0, keepdims=True)
        outs.append(acc)
    o_ref[...] = jnp.concatenate(outs, axis=0)


def dilated_decode(z, cache_k, cache_v):
    B, Tc = cache_k.shape[:2]
    Q = z.shape[0] // B
    W = C_WIDTH
    (w1, d1), (w4, d4), (w16, d16) = DIL_PATTERNS
    assert d1 == 1 and all(w // d == DIL_TQ and Tc % (d * DIL_TQ) == 0 and (d == 1 or Q <= d) for w, d in DIL_PATTERNS)
    assert Q <= DIL_TQ and len(DIL_PATTERNS) == 3
    NR = 16
    zv = jnp.pad(z.reshape(B, Q, 3 * W), ((0, 0), (0, NR - Q), (0, 0)))
    new = lambda c: pl.BlockSpec((None, NR, W), lambda b: (b, 0, c))
    view = lambda a, d: a.reshape(B, Tc // d, d * W)
    last = lambda d: pl.BlockSpec((None, DIL_TQ, min(d, Q) * W), lambda b: (b, Tc // d // DIL_TQ - 1, 0))
    head = jnp.arange(W)[:, None] // HEAD_DIM == jnp.arange(LANES)[None, :]
    out = pl.pallas_call(
        _dil_decode_body,
        grid=(B,),
        in_specs=[new(0), new(1), new(2), last(d1), last(d4), last(d16), last(d1), last(d4), last(d16),
                  pl.BlockSpec((W, LANES), lambda b: (0, 0)), pl.BlockSpec((LANES, W), lambda b: (0, 0))],
        out_specs=pl.BlockSpec((None, Q, W), lambda b: (b, 0, 0)),
        out_shape=jax.ShapeDtypeStruct((B, Q, W), F32),
        compiler_params=_cparams("parallel"),
        name="dilated_decode",
    )(zv, zv, zv, view(cache_k, d1), view(cache_k, d4), view(cache_k, d16),
      view(cache_v, d1), view(cache_v, d4), view(cache_v, d16), head.astype(BF16), head.T.astype(BF16))
    return out.reshape(B * Q, W)


def odd_mixer_sample(h, p0, past, P, o):
    buf_k, buf_v = past
    B, T, D = h.shape
    assert buf_k.shape[1] == C_WINDOW
    cos, sin = _rope_tables(jnp.tile(p0 + jnp.arange(T), B))
    z = inproj_odd(h.reshape(B * T, D), P['w_in_odd'][o], P['dil_q_gain'][o], P['dil_k_gain'][o], cos, sin)
    att = dilated_decode(z, buf_k, buf_v)
    mix = matmul(att, P['w_out_odd'][o])
    shp = (B, T, C_HEADS, HEAD_DIM)
    k = z[:, C_WIDTH:2 * C_WIDTH].reshape(shp)
    v = z[:, 2 * C_WIDTH:].reshape(shp)
    return mix.reshape(B, T, D), (jnp.concatenate([buf_k[:, T:], k], axis=1), jnp.concatenate([buf_v[:, T:], v], axis=1))


def rms_norm(x, g):
    xf = x.astype(jnp.float32)
    y = xf * lax.rsqrt(jnp.mean(xf * xf, axis=-1, keepdims=True) + EPS)
    return (y * g.astype(jnp.float32)).astype(x.dtype)


def rope(x, pos):
    half = HEAD_DIM // 2
    freq = ROPE_THETA ** (-jnp.arange(half, dtype=jnp.float32) / half)
    ang = pos.astype(jnp.float32)[:, None] * freq[None, :]
    cos = jnp.cos(ang)[None, :, None, :]
    sin = jnp.sin(ang)[None, :, None, :]
    xf = x.astype(jnp.float32)
    x1, x2 = xf[..., :half], xf[..., half:]
    return jnp.concatenate([x1 * cos - x2 * sin, x2 * cos + x1 * sin], axis=-1).astype(x.dtype)


def masked_softmax(s, mask):
    s = jnp.where(mask, s.astype(jnp.float32), NEG)
    m = jnp.max(s, axis=-1, keepdims=True)
    e = jnp.where(mask, jnp.exp(s - m), 0.0)
    den = jnp.maximum(jnp.sum(e, axis=-1, keepdims=True), TINY)
    return e / den, m + jnp.log(den)


def split_cols(z, sizes):
    out, o = [], 0
    for s in sizes:
        out.append(z[..., o:o + s])
        o += s
    return out


def over_query_blocks(fn, T):
    if T > Q_BLOCK and T % Q_BLOCK == 0:
        out = lax.map(lambda i: fn(i * Q_BLOCK, Q_BLOCK), jnp.arange(T // Q_BLOCK))
        return jnp.swapaxes(out, 0, 1).reshape(out.shape[1], T, out.shape[-1])
    return fn(0, T)


def gather_pages(pool, page_table):
    g = pool[page_table]
    return g.reshape(g.shape[0], g.shape[1] * g.shape[2], g.shape[3], g.shape[4])


def causal_conv(u, buf, w, b):
    T = u.shape[1]
    up = jnp.concatenate([buf, u], axis=1)
    y = b + up[:, 0:T] * w[0]
    for j in range(1, CONV_W):
        y = y + up[:, j:j + T] * w[j]
    return y, up[:, T:]


def _lin_comb(l, r):
    return (l[0] * r[0], r[0] * l[1] + r[1])


def rg_lru(u, h0, wa, ba, wi, bi, lam):
    B, T, W = u.shape
    ub = u.reshape(B, T, LRU_BLOCKS, LRU_BW)
    r = jax.nn.sigmoid((jnp.einsum('btki,kij->btkj', ub, wa).reshape(B, T, W) + ba).astype(jnp.float32))
    gi = jax.nn.sigmoid((jnp.einsum('btki,kij->btkj', ub, wi).reshape(B, T, W) + bi).astype(jnp.float32))
    log_a = -LRU_C * r * jax.nn.softplus(-lam.astype(jnp.float32))
    a = jnp.exp(log_a)
    b = jnp.sqrt(-jnp.expm1(2.0 * log_a)) * (gi * u.astype(jnp.float32))
    b = b.at[:, 0].add(a[:, 0] * h0.astype(jnp.float32))
    _, h = lax.associative_scan(_lin_comb, (a, b), axis=1)
    return h.astype(u.dtype), h[:, -1].astype(u.dtype)


def compress(rows, w, t_pad):
    B, T = rows.shape[:2]
    r = jnp.pad(rows, ((0, 0), (0, t_pad - T), (0, 0), (0, 0)))
    r = r.reshape(B, t_pad // CMP_LEN, CMP_LEN, NSA_KV, HEAD_DIM)
    return jnp.einsum('bnjgd,jde->bnge', r, w)


def sel_blocks(rows, t_pad):
    B, T = rows.shape[:2]
    r = jnp.pad(rows, ((0, 0), (0, t_pad - T), (0, 0), (0, 0)))
    return r.reshape(B, t_pad // SEL_LEN, SEL_LEN, NSA_KV, HEAD_DIM).transpose(0, 3, 1, 2, 4)


def nsa_attend(q, q_rot, gates, qpos, wstart, kc, vc, ksg, vsg, kw_p, vw_p):
    B, Q = q.shape[:2]
    G, M = NSA_KV, NSA_GROUP
    scale = HEAD_DIM ** -0.5
    qg = q.reshape(B, Q, G, M, HEAD_DIM)
    qr = q_rot.reshape(B, Q, G, M, HEAD_DIM)
    ncb = kc.shape[1]
    s_c = jnp.einsum('bqgmd,bngd->bgmqn', qg, kc) * scale
    cmask = ((jnp.arange(ncb) + 1) * CMP_LEN - 1)[None, :] <= qpos[:, None]
    p_c, _ = masked_softmax(s_c, cmask)
    o_c = jnp.einsum('bgmqn,bngd->bqgmd', p_c.astype(vc.dtype), vc)
    nsb = ksg.shape[2]
    imp = p_c.sum(axis=2).reshape(B, G, Q, nsb, CMP_PER_SEL).sum(-1)
    sb = jnp.arange(nsb)[None, :]
    cur = (qpos // SEL_LEN)[:, None]
    valid = sb * SEL_LEN <= qpos[:, None]
    forced = (sb == 0) | (sb == cur) | (sb == cur - 1)
    score = jnp.where(valid, imp + jnp.where(forced, FORCE, 0.0), NEG)
    top_s, top_i = lax.top_k(score, min(N_SEL, nsb))
    n = top_i.shape[-1]
    bi = jnp.arange(B)[:, None, None, None]
    gi = jnp.arange(G)[None, :, None, None]
    k_sel = ksg[bi, gi, top_i]
    v_sel = vsg[bi, gi, top_i]
    kpos = top_i[..., None] * SEL_LEN + jnp.arange(SEL_LEN)
    smask = (top_s > 0.5 * NEG)[..., None] & (kpos <= qpos[None, None, :, None, None])
    s_s = jnp.einsum('bqgmd,bgqnld->bgmqnl', qr, k_sel).reshape(B, G, M, Q, n * SEL_LEN) * scale
    p_s, _ = masked_softmax(s_s, smask.reshape(B, G, 1, Q, n * SEL_LEN))
    o_s = jnp.einsum('bgmqk,bgqkd->bqgmd', p_s.astype(v_sel.dtype), v_sel.reshape(B, G, Q, n * SEL_LEN, HEAD_DIM))
    span = SW_WIN + Q
    kwb = lax.dynamic_slice_in_dim(kw_p, wstart, span, axis=1)
    vwb = lax.dynamic_slice_in_dim(vw_p, wstart, span, axis=1)
    qloc = wstart + jnp.arange(Q)
    kloc = wstart - SW_WIN + jnp.arange(span)
    dist = qloc[:, None] - kloc[None, :]
    wmask = (dist >= 0) & (dist <= SW_WIN) & (kloc[None, :] >= 0)
    s_w = jnp.einsum('bqgmd,bkgd->bgmqk', qr, kwb) * scale
    p_w, _ = masked_softmax(s_w, wmask)
    o_w = jnp.einsum('bgmqk,bkgd->bqgmd', p_w.astype(vwb.dtype), vwb)
    gt = gates.reshape(B, Q, G, M, 3).astype(o_c.dtype)
    o = gt[..., 0:1] * o_c + gt[..., 1:2] * o_s + gt[..., 2:3] * o_w
    return o.reshape(B, Q, NSA_HEADS * HEAD_DIM)


def even_mixer(h, p0, past, P, e):
    lru_h0, conv_buf, past_ck, past_cv, past_sk, past_sv, win_k, win_v = past
    B, T, _ = h.shape
    z = matmul(h, P['w_in_even'][e])
    u, y, q, ck, cv, sk, sv, wk, wv, g = split_cols(z, EVEN_SIZES)
    uc, conv_new = causal_conv(u, conv_buf, P['conv_w'][e], P['conv_b'][e])
    hl, h_last = rg_lru(uc, lru_h0, P['lru_wa'][e], P['lru_ba'][e], P['lru_wi'][e], P['lru_bi'][e], P['lru_lambda'][e])
    lru_out = hl * jax.nn.gelu(y)
    pos = p0 + jnp.arange(T)
    kg = P['nsa_k_gain'][e]
    kvs = (B, T, NSA_KV, HEAD_DIM)
    q = rms_norm(q.reshape(B, T, NSA_HEADS, HEAD_DIM), P['nsa_q_gain'][e])
    q_rot = rope(q, pos)
    ck = rms_norm(ck.reshape(kvs), kg[0])
    cv = cv.reshape(kvs)
    sk = rope(rms_norm(sk.reshape(kvs), kg[1]), pos)
    sv = sv.reshape(kvs)
    wk = rope(rms_norm(wk.reshape(kvs), kg[2]), pos)
    wv = wv.reshape(kvs)
    gates = jax.nn.sigmoid(g.astype(jnp.float32)).reshape(B, T, NSA_HEADS, 3)
    ck_all = jnp.concatenate([past_ck, ck], axis=1)
    cv_all = jnp.concatenate([past_cv, cv], axis=1)
    sk_all = jnp.concatenate([past_sk, sk], axis=1)
    sv_all = jnp.concatenate([past_sv, sv], axis=1)
    t_k = ck_all.shape[1]
    t_pad = -(-t_k // SEL_LEN) * SEL_LEN
    kc = compress(ck_all, P['w_cmp_k'][e], t_pad)
    vc = compress(cv_all, P['w_cmp_v'][e], t_pad)
    ksg = sel_blocks(sk_all, t_pad)
    vsg = sel_blocks(sv_all, t_pad)
    wk_all = jnp.concatenate([win_k, wk], axis=1)
    wv_all = jnp.concatenate([win_v, wv], axis=1)
    wb = win_k.shape[1]
    pad_w = ((0, 0), (SW_WIN, 0), (0, 0), (0, 0))
    wk_p = jnp.pad(wk_all, pad_w)
    wv_p = jnp.pad(wv_all, pad_w)

    def block(q0, qb):
        sl = lambda a: lax.dynamic_slice_in_dim(a, q0, qb, axis=1)
        qpos = p0 + q0 + jnp.arange(qb)
        return nsa_attend(sl(q), sl(q_rot), sl(gates), qpos, wb + q0, kc, vc, ksg, vsg, wk_p, wv_p)

    nsa_out = over_query_blocks(block, T)
    mix = matmul(jnp.concatenate([lru_out, nsa_out.astype(lru_out.dtype)], axis=-1), P['w_out_even'][e])
    keep = min(SW_WIN, wk_all.shape[1])
    return mix, (h_last, conv_new, ck, cv, sk, sv, wk_all[:, -keep:], wv_all[:, -keep:])


def dilated_attend(q, kp, vp, qloc):
    scale = HEAD_DIM ** -0.5
    outs, lses = [], []
    for w, d in DIL_PATTERNS:
        kl = qloc[:, None] - d * jnp.arange(w // d + 1)[None, :]
        kg = kp[:, kl + C_WINDOW]
        vg = vp[:, kl + C_WINDOW]
        s = jnp.einsum('bqhd,bqnhd->bhqn', q, kg) * scale
        p, lse = masked_softmax(s, kl >= 0)
        outs.append(jnp.einsum('bhqn,bqnhd->bqhd', p.astype(vg.dtype), vg))
        lses.append(lse)
    wts = jax.nn.softmax(jnp.stack(lses), axis=0)
    o = jnp.swapaxes(wts[0], 1, 2).astype(outs[0].dtype) * outs[0]
    for i in range(1, len(DIL_PATTERNS)):
        o = o + jnp.swapaxes(wts[i], 1, 2).astype(outs[i].dtype) * outs[i]
    return o.reshape(o.shape[0], o.shape[1], C_WIDTH)


def odd_mixer(h, p0, past, P, o):
    buf_k, buf_v = past
    B, T, _ = h.shape
    z = matmul(h, P['w_in_odd'][o])
    q, k, v = split_cols(z, (C_WIDTH,) * 3)
    shp = (B, T, C_HEADS, HEAD_DIM)
    pos = p0 + jnp.arange(T)
    q = rope(rms_norm(q.reshape(shp), P['dil_q_gain'][o]), pos)
    k = rope(rms_norm(k.reshape(shp), P['dil_k_gain'][o]), pos)
    v = v.reshape(shp)
    k_all = jnp.concatenate([buf_k, k], axis=1)
    v_all = jnp.concatenate([buf_v, v], axis=1)
    wb = buf_k.shape[1]
    pad = ((0, 0), (C_WINDOW, 0), (0, 0), (0, 0))
    kp = jnp.pad(k_all, pad)
    vp = jnp.pad(v_all, pad)

    def block(q0, qb):
        return dilated_attend(lax.dynamic_slice_in_dim(q, q0, qb, axis=1), kp, vp, wb + q0 + jnp.arange(qb))

    att = over_query_blocks(block, T)
    keep = min(C_WINDOW, k_all.shape[1])
    return matmul(att, P['w_out_odd'][o]), (k_all[:, -keep:], v_all[:, -keep:])


def _expert_ffn_body(blk_e_ref, n_used_ref, x_ref, wg_ref, wu_ref, wd_ref, o_ref):
    @pl.when(pl.program_id(0) < n_used_ref[0])
    def _():
        x = x_ref[...].astype(BF16)
        g = jnp.dot(x, wg_ref[...].astype(BF16), preferred_element_type=F32)
        u = jnp.dot(x, wu_ref[...].astype(BF16), preferred_element_type=F32)
        hid = (g * jax.nn.sigmoid(g)) * u
        o_ref[...] = jnp.dot(hid.astype(BF16), wd_ref[...].astype(BF16), preferred_element_type=F32)

    @pl.when(pl.program_id(0) >= n_used_ref[0])
    def _():
        o_ref[...] = jnp.zeros_like(o_ref)


def expert_ffn(xg, blk_e, n_used, w_gate, w_up, w_down, rows):
    n_slots, D = xg.shape
    DE = w_gate.shape[-1]
    wspec = lambda a, b: pl.BlockSpec((None, a, b), lambda j, be, nu: (be[j], 0, 0))
    return pl.pallas_call(
        _expert_ffn_body,
        grid_spec=pltpu.PrefetchScalarGridSpec(
            num_scalar_prefetch=2,
            grid=(n_slots // rows,),
            in_specs=[pl.BlockSpec((rows, D), lambda j, be, nu: (j, 0)), wspec(D, DE), wspec(D, DE), wspec(DE, D)],
            out_specs=pl.BlockSpec((rows, D), lambda j, be, nu: (j, 0))),
        out_shape=jax.ShapeDtypeStruct((n_slots, D), F32),
        compiler_params=_cparams("arbitrary"),
        name="expert_ffn",
    )(blk_e, n_used, xg, w_gate, w_up, w_down)


def routed_experts(xf, eid, wts, w_gate, w_up, w_down):
    N, D = xf.shape
    M = N * TOP_K
    rows = 256 if M >= 256 * N_EXPERTS else MOE_BLK
    e_flat = eid.reshape(M)
    order = jnp.argsort(e_flat)
    e_sorted = e_flat[order]
    counts = jnp.bincount(e_flat, length=N_EXPERTS)
    padded = (counts + rows - 1) // rows * rows
    start = jnp.cumsum(counts) - counts
    pend = jnp.cumsum(padded)
    pstart = pend - padded
    dest = (pstart[e_sorted] + (jnp.arange(M) - start[e_sorted])).astype(jnp.int32)
    n_blocks = -(-(M + N_EXPERTS * (rows - 1)) // rows)
    n_slots = n_blocks * rows
    slot_tok = jnp.zeros((n_slots,), jnp.int32).at[dest].set((order // TOP_K).astype(jnp.int32))
    blk_e = jnp.minimum(jnp.searchsorted(pend, jnp.arange(n_blocks) * rows, side='right'), N_EXPERTS - 1)
    n_used = (pend[-1:] // rows).astype(jnp.int32)
    yb = expert_ffn(xf[slot_tok], blk_e.astype(jnp.int32), n_used, w_gate, w_up, w_down, rows)
    pos = jnp.zeros((M,), jnp.int32).at[order].set(dest).reshape(N, TOP_K)
    y = yb[pos[:, 0]] * wts[:, 0:1].astype(xf.dtype)
    for k in range(1, TOP_K):
        y = y + yb[pos[:, k]] * wts[:, k:k + 1].astype(xf.dtype)
    return y


def hier_moe(x, P, layer):
    B, T, D = x.shape
    N = B * T
    xf = x.reshape(N, D)
    rows = jnp.arange(N)
    w_router = jnp.concatenate([P['w_router_group'][layer], P['w_router_exp'][layer]], axis=1)
    logits = matmul(xf, jnp.pad(w_router, ((0, 0), (0, LANES - N_GROUPS - N_EXPERTS))))
    g_logit = (logits[:, :N_GROUPS] + P['b_router_group'][layer]).astype(jnp.float32)
    g_top = jnp.argmax(g_logit, axis=-1)
    g_w = jax.nn.softmax(g_logit, axis=-1)[rows, g_top]
    e_logit = (logits[:, N_GROUPS:N_GROUPS + N_EXPERTS] + P['b_router_exp'][layer]).astype(jnp.float32)
    e_logit = e_logit.reshape(N, N_GROUPS, EXP_PER_GROUP)[rows, g_top]
    top_l, top_i = lax.top_k(e_logit, TOP_K)
    top_p = jax.nn.softmax(top_l, axis=-1)
    eid = (g_top[:, None] * EXP_PER_GROUP + top_i).astype(jnp.int32)
    wts = g_w[:, None] * top_p
    y = routed_experts(xf, eid, wts, P['w_exp_gate'][layer], P['w_exp_up'][layer], P['w_exp_down'][layer])
    return y.reshape(B, T, D)


def trunk(x, p0, even_past, odd_past, P):
    even_new, odd_new = [], []
    for layer in range(DEPTH):
        hn = rms_norm(x, P['norm_mix'][layer])
        if layer % 2 == 0:
            if even_past is None:
                mix, st = even_mixer_prompt(hn, P, layer // 2)
            else:
                mix, st = even_mixer(hn, p0, even_past[layer // 2], P, layer // 2)
            even_new.append(st)
        else:
            if odd_past is None:
                mix, st = odd_mixer_prompt(hn, P, layer // 2)
            else:
                mix, st = odd_mixer_sample(hn, p0, odd_past[layer // 2], P, layer // 2)
            odd_new.append(st)
        x = x + mix.astype(x.dtype)
        x = x + hier_moe(rms_norm(x, P['norm_ffn'][layer]), P, layer).astype(x.dtype)
    return x, even_new, odd_new


def stack_layers(states):
    return [jnp.stack(items) for items in zip(*states)]


def kernel(x_prompt, x_sample, state_lru_h, state_lru_conv, cache_cmp_k, cache_cmp_v, cache_sel_k, cache_sel_v, cache_win_k, cache_win_v, cache_dil_k, cache_dil_v, page_table, norm_mix, norm_ffn, w_in_even, conv_w, conv_b, lru_wa, lru_ba, lru_wi, lru_bi, lru_lambda, nsa_q_gain, nsa_k_gain, w_cmp_k, w_cmp_v, w_out_even, w_in_odd, dil_q_gain, dil_k_gain, w_out_odd, w_router_group, b_router_group, w_router_exp, b_router_exp, w_exp_gate, w_exp_up, w_exp_down):
    P = dict(norm_mix=norm_mix, norm_ffn=norm_ffn, w_in_even=w_in_even, conv_w=conv_w, conv_b=conv_b,
             lru_wa=lru_wa, lru_ba=lru_ba, lru_wi=lru_wi, lru_bi=lru_bi, lru_lambda=lru_lambda,
             nsa_q_gain=nsa_q_gain, nsa_k_gain=nsa_k_gain, w_cmp_k=w_cmp_k, w_cmp_v=w_cmp_v,
             w_out_even=w_out_even, w_in_odd=w_in_odd, dil_q_gain=dil_q_gain, dil_k_gain=dil_k_gain,
             w_out_odd=w_out_odd, w_router_group=w_router_group, b_router_group=b_router_group,
             w_router_exp=w_router_exp, b_router_exp=b_router_exp, w_exp_gate=w_exp_gate,
             w_exp_up=w_exp_up, w_exp_down=w_exp_down)
    y_prompt, ev_p, od_p = trunk(x_prompt, 0, None, None, P)
    p0 = page_table.shape[1] * PAGE_SIZE
    ev1 = [(state_lru_h[e], state_lru_conv[e],
            gather_pages(cache_cmp_k[e], page_table), gather_pages(cache_cmp_v[e], page_table),
            gather_pages(cache_sel_k[e], page_table), gather_pages(cache_sel_v[e], page_table),
            cache_win_k[e], cache_win_v[e]) for e in range(N_EVEN)]
    od1 = [(cache_dil_k[o], cache_dil_v[o]) for o in range(N_ODD)]
    y_sample, ev_s, od_s = trunk(x_sample, p0, ev1, od1, P)
    p_lru_h, p_lru_conv, p_cmp_k, p_cmp_v, p_sel_k, p_sel_v, p_win_k, p_win_v = stack_layers(ev_p)
    p_dil_k, p_dil_v = stack_layers(od_p)
    s_lru_h, s_lru_conv, s_cmp_k, s_cmp_v, s_sel_k, s_sel_v, s_win_k, s_win_v = stack_layers(ev_s)
    s_dil_k, s_dil_v = stack_layers(od_s)
    return (y_prompt, y_sample, p_lru_h, p_lru_conv, p_cmp_k, p_cmp_v, p_sel_k, p_sel_v, p_win_k, p_win_v, p_dil_k, p_dil_v, s_lru_h, s_lru_conv, s_cmp_k, s_cmp_v, s_sel_k, s_sel_v, s_win_k, s_win_v, s_dil_k, s_dil_v)
```

```python
import functools

import jax, jax.numpy as jnp
from jax import lax
import numpy as np
from jax.experimental import pallas as pl
from jax.experimental.pallas import tpu as pltpu

D_MODEL = 1024
BATCH = 2
SEQ = 8192
DEPTH = 2
DEC_BATCH = 128
DEC_SEQ = 4
PAST_LEN = 2048
PAGE_SIZE = 128

HEAD_DIM = 64
LRU_WIDTH = D_MODEL // 2
LRU_BLOCKS = LRU_WIDTH // HEAD_DIM
LRU_BW = LRU_WIDTH // LRU_BLOCKS
CONV_W = 4
LRU_C = 8.0
NSA_HEADS = (D_MODEL // 2) // HEAD_DIM
NSA_KV = 2
NSA_GROUP = NSA_HEADS // NSA_KV
CMP_LEN = 32
SEL_LEN = 64
CMP_PER_SEL = SEL_LEN // CMP_LEN
N_SEL = 16
SW_WIN = 512
FORCE = 1000.0
C_HEADS = D_MODEL // HEAD_DIM
DIL_PATTERNS = ((128, 1), (512, 4), (2048, 16))
C_WINDOW = 2048
N_GROUPS = 4
EXP_PER_GROUP = 8
N_EXPERTS = N_GROUPS * EXP_PER_GROUP
D_EXPERT = D_MODEL // 2
TOP_K = 2
MOE_BLK = 128
Q_BLOCK = 128
ROPE_THETA = 10000.0
EPS = 1e-6
NEG = -1e30
TINY = 1e-30
N_EVEN = (DEPTH + 1) // 2
N_ODD = DEPTH // 2
EVEN_SIZES = (LRU_WIDTH, LRU_WIDTH, NSA_HEADS * HEAD_DIM) + (NSA_KV * HEAD_DIM,) * 6 + (3 * NSA_HEADS,)
EVEN_COLS = sum(EVEN_SIZES)
EVEN_OUT = LRU_WIDTH + NSA_HEADS * HEAD_DIM
C_WIDTH = C_HEADS * HEAD_DIM


def _mm_body(x_ref, w_ref, o_ref):
    o_ref[...] = jnp.dot(x_ref[...].astype(jnp.bfloat16), w_ref[...].astype(jnp.bfloat16),
                         preferred_element_type=jnp.float32)


def matmul(x, w, tm=512, tn=512):
    lead = x.shape[:-1]
    K = x.shape[-1]
    M = w.shape[-1]
    x2 = x.reshape(-1, K)
    N = x2.shape[0]
    tm = min(tm, N)
    tn = min(tn, M)
    out = pl.pallas_call(
        _mm_body,
        grid=(pl.cdiv(N, tm), pl.cdiv(M, tn)),
        in_specs=[pl.BlockSpec((tm, K), lambda i, j: (i, 0)),
                  pl.BlockSpec((K, tn), lambda i, j: (0, j))],
        out_specs=pl.BlockSpec((tm, tn), lambda i, j: (i, j)),
        out_shape=jax.ShapeDtypeStruct((N, M), jnp.float32),
        name="matmul",
    )(x2, w)
    return out.reshape(lead + (M,))


LANES = 128
VMEM_LIMIT = 48 * 1024 * 1024
BF16 = jnp.bfloat16
F32 = jnp.float32


def _cparams(*sem):
    return pltpu.CompilerParams(dimension_semantics=sem, vmem_limit_bytes=VMEM_LIMIT)


def _rope_tables(pos):
    half = HEAD_DIM // 2
    freq = ROPE_THETA ** (-jnp.arange(half, dtype=F32) / half)
    ang = pos.astype(F32)[:, None] * freq[None, :]
    cos, sin = jnp.cos(ang), jnp.sin(ang)
    return jnp.tile(cos, (1, 4)), jnp.tile(jnp.concatenate([-sin, sin], axis=1), (1, 2))


def _head_mean_matrix(width):
    h = jnp.arange(width) // HEAD_DIM
    return jnp.where(h[:, None] == h[None, :], 1.0 / HEAD_DIM, 0.0).astype(BF16)


def _head_norm(x, gain, gmat):
    x2 = x * x
    hi = x2.astype(BF16)
    lo = (x2 - hi.astype(F32)).astype(BF16)
    ms = jnp.dot(hi, gmat, preferred_element_type=F32) + jnp.dot(lo, gmat, preferred_element_type=F32)
    return x * lax.rsqrt(ms + EPS) * gain


def _rope_lanes(y, cos, sin):
    hi_half = (lax.broadcasted_iota(jnp.int32, (y.shape[0], LANES), 1) & (HEAD_DIM // 2)) != 0
    outs = []
    for c in range(y.shape[1] // LANES):
        yc = y[:, c * LANES:(c + 1) * LANES]
        partner = jnp.where(hi_half, pltpu.roll(yc, HEAD_DIM // 2, 1), pltpu.roll(yc, LANES - HEAD_DIM // 2, 1))
        outs.append(yc * cos + partner * sin)
    return outs[0] if len(outs) == 1 else jnp.concatenate(outs, axis=1)


def _inproj_odd_body(x_ref, w_ref, cos_ref, sin_ref, gain_ref, gmat_ref, o_ref, *, n_rope):
    j = pl.program_id(1)
    acc = jnp.dot(x_ref[...].astype(BF16), w_ref[...].astype(BF16), preferred_element_type=F32)

    @pl.when(j < n_rope)
    def _():
        o_ref[...] = _rope_lanes(_head_norm(acc, gain_ref[0], gmat_ref[...]), cos_ref[...], sin_ref[...])

    @pl.when(j >= n_rope)
    def _():
        o_ref[...] = acc


def inproj_odd(h2, w, q_gain, k_gain, cos, sin, tm=512, tn=512):
    N, K = h2.shape
    M = w.shape[-1]
    tm = min(tm, N)
    reps = tn // HEAD_DIM
    per = C_WIDTH // tn
    gains = jnp.concatenate([jnp.tile(jnp.tile(q_gain, reps)[None], (per, 1)),
                             jnp.tile(jnp.tile(k_gain, reps)[None], (per, 1)),
                             jnp.ones((per, tn), F32)], axis=0)[:, None, :]
    return pl.pallas_call(
        functools.partial(_inproj_odd_body, n_rope=2 * per),
        grid=(N // tm, M // tn),
        in_specs=[pl.BlockSpec((tm, K), lambda i, j: (i, 0)),
                  pl.BlockSpec((K, tn), lambda i, j: (0, j)),
                  pl.BlockSpec((tm, LANES), lambda i, j: (i, 0)),
                  pl.BlockSpec((tm, LANES), lambda i, j: (i, 0)),
                  pl.BlockSpec((1, 1, tn), lambda i, j: (j, 0, 0)),
                  pl.BlockSpec((tn, tn), lambda i, j: (0, 0))],
        out_specs=pl.BlockSpec((tm, tn), lambda i, j: (i, j)),
        out_shape=jax.ShapeDtypeStruct((N, M), F32),
        compiler_params=_cparams("parallel", "arbitrary"),
        name="inproj_odd",
    )(h2, w, cos, sin, gains, _head_mean_matrix(tn))


DIL_TQ = 128


def _dil_body(q_ref, kc_ref, kp_ref, vc_ref, vp_ref, o_ref, lse_ref):
    i = pl.program_id(2)
    tq = q_ref.shape[0]
    u = lax.broadcasted_iota(jnp.int32, (2 * tq, 2 * tq), 0) & (tq - 1)
    c = lax.broadcasted_iota(jnp.int32, (2 * tq, 2 * tq), 1)
    in_prev = jnp.logical_and(jnp.logical_and(c < tq, c >= u), i > 0)
    mask = jnp.logical_or(in_prev, jnp.logical_and(c >= tq, c - tq <= u))
    low = lax.broadcasted_iota(jnp.int32, (tq, LANES), 1) < HEAD_DIM
    scale = HEAD_DIM ** -0.5
    dn = (((1,), (1,)), ((), ()))
    for hp in range(q_ref.shape[1] // LANES):
        sl = slice(hp * LANES, (hp + 1) * LANES)
        q2 = q_ref[:, sl] * scale
        qs = jnp.concatenate([jnp.where(low, q2, 0.0), jnp.where(low, 0.0, q2)], axis=0).astype(BF16)
        kk = jnp.concatenate([kp_ref[:, sl], kc_ref[:, sl]], axis=0).astype(BF16)
        vv = jnp.concatenate([vp_ref[:, sl], vc_ref[:, sl]], axis=0).astype(BF16)
        s = jnp.where(mask, lax.dot_general(qs, kk, dn, preferred_element_type=F32), NEG)
        m = jnp.max(s, axis=-1, keepdims=True)
        e = jnp.exp(s - m)
        den = jnp.maximum(jnp.sum(e, axis=-1, keepdims=True), TINY)
        o = jnp.dot(e.astype(BF16), vv, preferred_element_type=F32) / den
        lse = jnp.broadcast_to(m + jnp.log(den), (2 * tq, LANES))
        o_ref[:, sl] = jnp.where(low, o[:tq], o[tq:])
        lse_ref[:, sl] = jnp.where(low, lse[:tq], lse[tq:])


def dilated_pattern(z, B, T, d):
    W = C_WIDTH
    Td = T // d
    zv = z.reshape(B, Td, d * 3 * W)
    blk = lambda off, prev: pl.BlockSpec(
        (None, DIL_TQ, W), (lambda b, r, i: (b, jnp.maximum(i - 1, 0), 3 * r + off)) if prev
        else (lambda b, r, i: (b, i, 3 * r + off)))
    o_spec = pl.BlockSpec((None, DIL_TQ, W), lambda b, r, i: (b, i, r))
    out, lse = pl.pallas_call(
        _dil_body,
        grid=(B, d, Td // DIL_TQ),
        in_specs=[blk(0, False), blk(1, False), blk(1, True), blk(2, False), blk(2, True)],
        out_specs=[o_spec, o_spec],
        out_shape=[jax.ShapeDtypeStruct((B, Td, d * W), F32)] * 2,
        compiler_params=_cparams("parallel", "parallel", "arbitrary"),
        name="dilated_d%d" % d,
    )(zv, zv, zv, zv, zv)
    return out.reshape(B * T, W), lse.reshape(B * T, W)


def _combine_outproj_body(o1, o2, o3, l1, l2, l3, w_ref, out_ref):
    m = jnp.maximum(jnp.maximum(l1[...], l2[...]), l3[...])
    e1, e2, e3 = jnp.exp(l1[...] - m), jnp.exp(l2[...] - m), jnp.exp(l3[...] - m)
    den = e1 + e2 + e3
    att = (e1 / den) * o1[...] + (e2 / den) * o2[...] + (e3 / den) * o3[...]
    out_ref[...] = jnp.dot(att.astype(BF16), w_ref[...].astype(BF16), preferred_element_type=F32)


def combine_outproj(outs, lses, w, tm=256):
    N, W = outs[0].shape
    M = w.shape[-1]
    row = pl.BlockSpec((tm, W), lambda i: (i, 0))
    return pl.pallas_call(
        _combine_outproj_body,
        grid=(N // tm,),
        in_specs=[row] * 6 + [pl.BlockSpec((W, M), lambda i: (0, 0))],
        out_specs=pl.BlockSpec((tm, M), lambda i: (i, 0)),
        out_shape=jax.ShapeDtypeStruct((N, M), F32),
        compiler_params=_cparams("parallel"),
        name="dil_combine_outproj",
    )(*outs, *lses, w)


def odd_mixer_prompt(h, P, o):
    B, T, D = h.shape
    assert all(w // d == DIL_TQ and T % (d * DIL_TQ) == 0 for w, d in DIL_PATTERNS)
    cos, sin = _rope_tables(jnp.tile(jnp.arange(T), B))
    z = inproj_odd(h.reshape(B * T, D), P['w_in_odd'][o], P['dil_q_gain'][o], P['dil_k_gain'][o], cos, sin)
    res = [dilated_pattern(z, B, T, d) for _, d in DIL_PATTERNS]
    mix = combine_outproj([r[0] for r in res], [r[1] for r in res], P['w_out_odd'][o])
    keep = min(C_WINDOW, T)
    shp = (B, T, C_HEADS, HEAD_DIM)
    k = z[:, C_WIDTH:2 * C_WIDTH].reshape(shp)
    v = z[:, 2 * C_WIDTH:].reshape(shp)
    return mix.reshape(B, T, D), (k[:, -keep:], v[:, -keep:])


EVEN_PAD = -(-EVEN_COLS // LANES) * LANES
NSA_TQ = 256
NSA_TK = 512
NSA_SPAN = SW_WIN + NSA_TQ
NSA_MAXBLK = LANES
SEL_OFF = -1e9
M_INIT = -3e38


def _dup_group(x, low):
    r = pltpu.roll(x, HEAD_DIM, 1)
    return jnp.where(low, x, r), jnp.where(low, r, x)


def _nsa_prep_body(zq_ref, zkv_ref, zg_ref, cos_ref, sin_ref, qg_ref, kg_ref, gmat_ref,
                   qn_ref, qr_ref, ck_ref, sk_ref, wk_ref, skd_ref, svd_ref, wkd_ref, wvd_ref, gt_ref):
    cos, sin = cos_ref[...], sin_ref[...]
    low = lax.broadcasted_iota(jnp.int32, (zq_ref.shape[0], LANES), 1) < HEAD_DIM
    qn = _head_norm(zq_ref[...], qg_ref[...], gmat_ref[...])
    qn_ref[...] = qn
    qr_ref[...] = _rope_lanes(qn, cos, sin)
    g1 = gmat_ref[0:LANES, 0:LANES]
    col = lambda c: zkv_ref[:, c * LANES:(c + 1) * LANES]
    ck_ref[...] = _head_norm(col(0), kg_ref[0:1, :], g1)
    sk = _rope_lanes(_head_norm(col(2), kg_ref[1:2, :], g1), cos, sin)
    wk = _rope_lanes(_head_norm(col(4), kg_ref[2:3, :], g1), cos, sin)
    sk_ref[...] = sk
    wk_ref[...] = wk
    for x, ref in ((sk, skd_ref), (col(3), svd_ref), (wk, wkd_ref), (col(5), wvd_ref)):
        d0, d1 = _dup_group(x, low)
        ref[0] = d0.astype(BF16)
        ref[1] = d1.astype(BF16)
    gt = 1.0 / (1.0 + jnp.exp(-zg_ref[...]))
    gt_ref[0] = gt
    gt_ref[1] = pltpu.roll(gt, LANES - 3 * NSA_GROUP, 1)


def nsa_prep(z, cos, sin, q_gain, k_gain, tm=256):
    N = z.shape[0]
    tm = min(tm, N)
    QW = NSA_HEADS * HEAD_DIM
    KW = NSA_KV * HEAD_DIM
    assert KW == LANES and QW % LANES == 0 and (LRU_WIDTH * 2) % QW == 0
    f = lambda shape: jax.ShapeDtypeStruct(shape, F32)
    b = lambda shape: jax.ShapeDtypeStruct(shape, BF16)
    row = lambda w: pl.BlockSpec((tm, w), lambda i: (i, 0))
    grp = pl.BlockSpec((NSA_KV, tm, LANES), lambda i: (0, i, 0))
    return pl.pallas_call(
        _nsa_prep_body,
        grid=(N // tm,),
        in_specs=[pl.BlockSpec((tm, QW), lambda i: (i, 2 * LRU_WIDTH // QW)),
                  pl.BlockSpec((tm, 6 * KW), lambda i: (i, (2 * LRU_WIDTH + QW) // (6 * KW))),
                  pl.BlockSpec((tm, LANES), lambda i: (i, (2 * LRU_WIDTH + QW + 6 * KW) // LANES)),
                  row(LANES), row(LANES),
                  pl.BlockSpec((1, QW), lambda i: (0, 0)),
                  pl.BlockSpec((3, LANES), lambda i: (0, 0)),
                  pl.BlockSpec((QW, QW), lambda i: (0, 0))],
        out_specs=[row(QW), row(QW), row(LANES), row(LANES), row(LANES), grp, grp, grp, grp, grp],
        out_shape=[f((N, QW)), f((N, QW)), f((N, LANES)), f((N, LANES)), f((N, LANES)),
                   b((NSA_KV, N, LANES)), b((NSA_KV, N, LANES)), b((NSA_KV, N, LANES)), b((NSA_KV, N, LANES)),
                   f((NSA_KV, N, LANES))],
        compiler_params=_cparams("parallel"),
        name="nsa_prep",
    )(z, z, z, cos, sin, jnp.tile(q_gain, NSA_HEADS)[None], jnp.tile(k_gain, (1, NSA_KV)), _head_mean_matrix(QW))


def compress_blocks(rows, w, B, T):
    nsb = T // SEL_LEN
    wc = jnp.einsum('jde,gh->jgdhe', w, jnp.eye(NSA_KV, dtype=w.dtype)).reshape(CMP_LEN * LANES, LANES)
    kc = matmul(rows.reshape(B * T // CMP_LEN, CMP_LEN * LANES), wc, tm=256)
    kc = kc.reshape(B, nsb, CMP_PER_SEL, NSA_KV, HEAD_DIM).transpose(0, 3, 2, 1, 4)
    kc = jnp.pad(kc, ((0, 0), (0, 0), (0, 0), (0, NSA_MAXBLK - nsb), (0, 0)))
    kc = kc.reshape(B, NSA_KV, CMP_PER_SEL * NSA_MAXBLK, HEAD_DIM)
    return jnp.concatenate([kc, kc], axis=-1)


def _flash_step(s, v, m_, l_, acc):
    m_new = jnp.maximum(m_, jnp.max(s, axis=-1, keepdims=True))
    alpha = jnp.exp(m_ - m_new)
    p = jnp.exp(s - m_new)
    l_new = alpha * l_ + jnp.sum(p, axis=-1, keepdims=True)
    return m_new, l_new, alpha * acc + jnp.dot(p.astype(BF16), v, preferred_element_type=F32)


def _nsa_body(qn_ref, qr_ref, gate_ref, kc_ref, vc_ref, sk_ref, sv_ref, wk_ref, wv_ref, e_ref, o_ref):
    i = pl.program_id(2)
    tq = NSA_TQ
    R = NSA_GROUP * tq
    q0 = i * tq
    scale = HEAD_DIM ** -0.5
    dn = (((1,), (1,)), ((), ()))
    lane = lax.broadcasted_iota(jnp.int32, (tq, LANES), 1)
    low = lane < HEAD_DIM
    qpos1 = q0 + lax.broadcasted_iota(jnp.int32, (tq, 1), 0)
    qpos = q0 + (lax.broadcasted_iota(jnp.int32, (R, 1), 0) & (tq - 1))
    unstack = lambda x: [x[m * tq:(m + 1) * tq] for m in range(NSA_GROUP)]

    def stack(ref):
        parts = []
        for m in range(NSA_GROUP):
            pair, half = divmod(m, 2)
            x = ref[:, pair * LANES:(pair + 1) * LANES] * scale
            parts.append(jnp.where(low if half == 0 else jnp.logical_not(low), x, 0.0))
        return jnp.concatenate(parts, axis=0).astype(BF16)

    cidx = lax.broadcasted_iota(jnp.int32, (R, CMP_PER_SEL * NSA_MAXBLK), 1)
    par = (cidx >= NSA_MAXBLK).astype(jnp.int32)
    blk = CMP_PER_SEL * (cidx - NSA_MAXBLK * par) + par
    cmask = (blk + 1) * CMP_LEN - 1 <= qpos
    s = jnp.where(cmask, lax.dot_general(stack(qn_ref), kc_ref[...].astype(BF16), dn, preferred_element_type=F32), NEG)
    mx = jnp.max(s, axis=-1, keepdims=True)
    e = jnp.where(cmask, jnp.exp(s - mx), 0.0)
    p = e / jnp.maximum(jnp.sum(e, axis=-1, keepdims=True), TINY)
    o_c = jnp.dot(p.astype(BF16), vc_ref[...].astype(BF16), preferred_element_type=F32)
    heads = unstack(p)
    psum = heads[0]
    for x in heads[1:]:
        psum = psum + x
    imp = psum[:, :NSA_MAXBLK] + psum[:, NSA_MAXBLK:]

    cur = lax.shift_right_logical(qpos1, SEL_LEN.bit_length() - 1)
    valid = lane * SEL_LEN <= qpos1
    forced = (lane == 0) | (lane == cur) | (lane == cur - 1)
    score = jnp.where(valid, imp + jnp.where(forced, FORCE, 0.0), NEG)
    lane_f = lane.astype(F32)

    def pick(_, carry):
        sc, sel = carry
        mx = jnp.max(sc, axis=-1, keepdims=True)
        first = jnp.min(jnp.where(sc == mx, lane_f, float(LANES)), axis=-1, keepdims=True)
        hit = lane_f == first
        sel = jnp.where(jnp.logical_and(hit, mx > 0.5 * NEG), 0.0, sel)
        return jnp.where(hit, M_INIT, sc), sel

    _, selneg = lax.fori_loop(0, N_SEL, pick, (score, jnp.full((tq, LANES), SEL_OFF, F32)), unroll=True)
    qr = stack(qr_ref)
    q_aug = jnp.concatenate([qr, jnp.concatenate([selneg.astype(BF16)] * NSA_GROUP, axis=0)], axis=1)

    def sweep(t, carry):
        ks = pl.multiple_of(t * NSA_TK, NSA_TK)
        k_aug = jnp.concatenate([sk_ref[pl.ds(ks, NSA_TK), :], e_ref[pl.ds(ks, NSA_TK), :]], axis=1)
        s = lax.dot_general(q_aug, k_aug, dn, preferred_element_type=F32)
        return _flash_step(s, sv_ref[pl.ds(ks, NSA_TK), :], *carry)

    n_full = q0 // NSA_TK
    init = (jnp.full((R, 1), M_INIT, F32), jnp.zeros((R, 1), F32), jnp.zeros((R, LANES), F32))
    carry = lax.fori_loop(0, n_full, sweep, init)
    ws = pl.multiple_of(jnp.maximum(q0 - SW_WIN, 0), NSA_TQ)
    kpos = ws + lax.broadcasted_iota(jnp.int32, (1, NSA_SPAN), 1)
    dist = qpos - kpos
    k_aug = jnp.concatenate([sk_ref[pl.ds(ws, NSA_SPAN), :], e_ref[pl.ds(ws, NSA_SPAN), :]], axis=1)
    s = jnp.where(jnp.logical_and(kpos >= n_full * NSA_TK, dist >= 0),
                  lax.dot_general(q_aug, k_aug, dn, preferred_element_type=F32), NEG)
    _, l_s, acc_s = _flash_step(s, sv_ref[pl.ds(ws, NSA_SPAN), :], *carry)
    o_s = acc_s / l_s

    s = jnp.where(jnp.logical_and(dist >= 0, dist <= SW_WIN),
                  lax.dot_general(qr, wk_ref[pl.ds(ws, NSA_SPAN), :], dn, preferred_element_type=F32), NEG)
    mx = jnp.max(s, axis=-1, keepdims=True)
    e = jnp.exp(s - mx)
    den = jnp.maximum(jnp.sum(e, axis=-1, keepdims=True), TINY)
    o_w = jnp.dot(e.astype(BF16), wv_ref[pl.ds(ws, NSA_SPAN), :], preferred_element_type=F32) / den

    gates = gate_ref[...]
    gate = lambda j: jnp.concatenate([gates[:, 3 * m + j:3 * m + j + 1] for m in range(NSA_GROUP)], axis=0)
    outs = unstack(gate(0) * o_c + gate(1) * o_s + gate(2) * o_w)
    for pair in range(NSA_GROUP // 2):
        o_ref[:, pair * LANES:(pair + 1) * LANES] = jnp.where(low, outs[2 * pair], outs[2 * pair + 1])


def nsa_attention(qn, qr, gates, kc, vc, skd, svd, wkd, wvd, B, T):
    assert T % NSA_TK == 0 and NSA_SPAN <= T <= NSA_MAXBLK * SEL_LEN and T // SEL_LEN >= N_SEL
    GW = NSA_GROUP * HEAD_DIM
    onehot = (jnp.arange(T)[:, None] // SEL_LEN == jnp.arange(NSA_MAXBLK)[None, :]).astype(BF16)
    qspec = pl.BlockSpec((None, NSA_TQ, GW), lambda b, g, i: (b, i, g))
    cspec = pl.BlockSpec((None, None, CMP_PER_SEL * NSA_MAXBLK, LANES), lambda b, g, i: (b, g, 0, 0))
    kspec = pl.BlockSpec((None, None, T, LANES), lambda b, g, i: (g, b, 0, 0))
    r4 = lambda a: a.reshape(NSA_KV, B, T, LANES)
    out = pl.pallas_call(
        _nsa_body,
        grid=(B, NSA_KV, T // NSA_TQ),
        in_specs=[qspec, qspec,
                  pl.BlockSpec((None, None, NSA_TQ, LANES), lambda b, g, i: (g, b, i, 0)),
                  cspec, cspec, kspec, kspec, kspec, kspec,
                  pl.BlockSpec((T, NSA_MAXBLK), lambda b, g, i: (0, 0))],
        out_specs=qspec,
        out_shape=jax.ShapeDtypeStruct((B, T, NSA_HEADS * HEAD_DIM), F32),
        compiler_params=_cparams("parallel", "parallel", "arbitrary"),
        name="nsa_attention",
    )(qn.reshape(B, T, -1), qr.reshape(B, T, -1), r4(gates), kc, vc, r4(skd), r4(svd), r4(wkd), r4(wvd), onehot)
    return out.reshape(B * T, NSA_HEADS * HEAD_DIM)


LRU_TM = 256
SUBLANES = 8


def _lru_body(u_ref, y_ref, cw_ref, cb_ref, wa_ref, ba_ref, wi_ref, bi_ref, sp_ref, h0_ref, cbuf_ref,
              o_ref, hlast_ref, ctail_ref, h_sc, tail_sc, a_sc, b_sc):
    tm = u_ref.shape[0]

    @pl.when(pl.program_id(1) == 0)
    def _():
        h_sc[...] = h0_ref[...]
        tail_sc[...] = cbuf_ref[...]

    u = u_ref[...]
    ext = jnp.concatenate([tail_sc[...], u], axis=0)
    uc = cb_ref[...]
    for j in range(CONV_W):
        lo = SUBLANES - (CONV_W - 1) + j
        uc = uc + ext[lo:lo + tm] * cw_ref[j:j + 1, :]
    tail_sc[...] = u[tm - SUBLANES:tm]
    ctail_ref[...] = u[tm - SUBLANES:tm]
    ub = uc.astype(BF16)
    r = jax.nn.sigmoid(jnp.dot(ub, wa_ref[...].astype(BF16), preferred_element_type=F32) + ba_ref[...])
    gi = jax.nn.sigmoid(jnp.dot(ub, wi_ref[...].astype(BF16), preferred_element_type=F32) + bi_ref[...])
    log_a = -LRU_C * r * sp_ref[...]
    a = jnp.exp(log_a)
    a_sc[...] = a
    b_sc[...] = jnp.sqrt(-jnp.tanh(log_a) * (a * a + 1.0)) * (gi * uc)

    def step(k, h):
        r0 = pl.multiple_of(k * SUBLANES, SUBLANES)
        a8 = a_sc[pl.ds(r0, SUBLANES), :]
        b8 = b_sc[pl.ds(r0, SUBLANES), :]
        rows = []
        for j in range(SUBLANES):
            h = a8[j:j + 1] * h + b8[j:j + 1]
            rows.append(h)
        b_sc[pl.ds(r0, SUBLANES), :] = jnp.concatenate(rows, axis=0)
        return h

    h = lax.fori_loop(0, tm // SUBLANES, step, h_sc[...])
    h_sc[...] = h
    hlast_ref[...] = h
    o_ref[...] = b_sc[...] * jax.nn.gelu(y_ref[...])


def rg_lru_prompt(z, B, T, h0, conv_buf, P, e):
    W = LRU_WIDTH
    tm = LRU_TM
    assert T % tm == 0 and tm >= SUBLANES and CONV_W - 1 <= SUBLANES
    nt = T // tm
    bd = lambda w: jnp.einsum('kij,kl->kilj', w, jnp.eye(LRU_BLOCKS, dtype=w.dtype)).reshape(W, W)
    vec = lambda v: v.reshape(1, W)
    cbuf = jnp.pad(conv_buf, ((0, 0), (SUBLANES - (CONV_W - 1), 0), (0, 0)))
    const = lambda shape: pl.BlockSpec(shape, lambda b, i: (0,) * len(shape))
    per_b = lambda rows: pl.BlockSpec((None, rows, W), lambda b, i: (b, 0, 0))
    out, h_last, ctail = pl.pallas_call(
        _lru_body,
        grid=(B, nt),
        in_specs=[pl.BlockSpec((tm, W), lambda b, i: (b * nt + i, 0)),
                  pl.BlockSpec((tm, W), lambda b, i: (b * nt + i, 1)),
                  const((CONV_W, W)), const((1, W)), const((W, W)), const((1, W)), const((W, W)), const((1, W)),
                  const((1, W)), per_b(1), per_b(SUBLANES)],
        out_specs=[pl.BlockSpec((tm, W), lambda b, i: (b * nt + i, 0)), per_b(1), per_b(SUBLANES)],
        out_shape=[jax.ShapeDtypeStruct((B * T, W), F32), jax.ShapeDtypeStruct((B, 1, W), F32),
                   jax.ShapeDtypeStruct((B, SUBLANES, W), F32)],
        scratch_shapes=[pltpu.VMEM((1, W), F32), pltpu.VMEM((SUBLANES, W), F32),
                        pltpu.VMEM((tm, W), F32), pltpu.VMEM((tm, W), F32)],
        compiler_params=_cparams("parallel", "arbitrary"),
        name="rg_lru",
    )(z, z, P['conv_w'][e], vec(P['conv_b'][e]), bd(P['lru_wa'][e]), vec(P['lru_ba'][e]), bd(P['lru_wi'][e]),
      vec(P['lru_bi'][e]), vec(jax.nn.softplus(-P['lru_lambda'][e].astype(F32))), h0[:, None, :], cbuf)
    return out, h_last[:, 0], ctail[:, SUBLANES - (CONV_W - 1):]


def even_mixer_prompt(h, P, e):
    B, T, D = h.shape
    N = B * T
    w_in = jnp.pad(P['w_in_even'][e], ((0, 0), (0, EVEN_PAD - EVEN_COLS)))
    z = matmul(h.reshape(N, D), w_in)
    lru_out, h_last, conv_new = rg_lru_prompt(z, B, T, jnp.zeros((B, LRU_WIDTH), F32),
                                              jnp.zeros((B, CONV_W - 1, LRU_WIDTH), F32), P, e)
    cos, sin = _rope_tables(jnp.tile(jnp.arange(T), B))
    qn, qr, ck, sk, wk, skd, svd, wkd, wvd, gates = nsa_prep(z, cos, sin, P['nsa_q_gain'][e], P['nsa_k_gain'][e])
    base = 2 * LRU_WIDTH + NSA_HEADS * HEAD_DIM
    kvcol = lambda c: z[:, base + c * LANES:base + (c + 1) * LANES]
    cv, sv, wv = kvcol(1), kvcol(3), kvcol(5)
    kc = compress_blocks(ck, P['w_cmp_k'][e], B, T)
    vc = compress_blocks(cv, P['w_cmp_v'][e], B, T)
    nsa_out = nsa_attention(qn, qr, gates, kc, vc, skd, svd, wkd, wvd, B, T)
    mix = matmul(jnp.concatenate([lru_out, nsa_out], axis=-1), P['w_out_even'][e])
    kvs = (B, T, NSA_KV, HEAD_DIM)
    keep = min(SW_WIN, T)
    r = lambda a: a.reshape(kvs)
    return mix.reshape(B, T, D), (h_last, conv_new, r(ck), r(cv), r(sk), r(sv), r(wk)[:, -keep:], r(wv)[:, -keep:])


def _bf16_round(x):
    return x.astype(BF16).astype(F32)


def _split_dot(x, m):
    hi = x.astype(BF16)
    lo = (x - hi.astype(F32)).astype(BF16)
    return jnp.dot(hi, m, preferred_element_type=F32) + jnp.dot(lo, m, preferred_element_type=F32)


def _dil_decode_body(q_ref, kn_ref, vn_ref, k1_ref, k4_ref, k16_ref, v1_ref, v4_ref, v16_ref, hsum_ref, hexp_ref, o_ref):
    nq = o_ref.shape[0]
    W = C_WIDTH
    scale = HEAD_DIM ** -0.5
    hsum = hsum_ref[...]
    hexp = hexp_ref[...]
    crow = lax.broadcasted_iota(jnp.int32, (DIL_TQ, 1), 0)
    nrow = lax.broadcasted_iota(jnp.int32, (q_ref.shape[0], 1), 0)
    kn = _bf16_round(kn_ref[...])
    vn = _bf16_round(vn_ref[...])
    k1 = _bf16_round(k1_ref[...])
    v1 = _bf16_round(v1_ref[...])
    outs = []
    for t in range(nq):
        qt = _bf16_round(q_ref[t:t + 1, :] * scale)
        cs = slice(t * W, (t + 1) * W)
        keys = (k1, _bf16_round(k4_ref[:, cs]), _bf16_round(k16_ref[:, cs]))
        vals = (v1, _bf16_round(v4_ref[:, cs]), _bf16_round(v16_ref[:, cs]))
        s = [_split_dot(k * qt, hsum) for k in keys]
        s[0] = jnp.where(crow >= t, s[0], NEG)
        s_new = _split_dot(kn * qt, hsum)
        mult = (nrow <= t).astype(F32) + (len(DIL_PATTERNS) - 1) * (nrow == t).astype(F32)
        s_new = jnp.where(nrow <= t, s_new, NEG)
        m = jnp.max(s_new, axis=0, keepdims=True)
        for x in s:
            m = jnp.maximum(m, jnp.max(x, axis=0, keepdims=True))
        e = [jnp.exp(x - m) for x in s]
        e_new = mult * jnp.exp(s_new - m)
        den = jnp.sum(e_new, axis=0, keepdims=True)
        for x in e:
            den = den + jnp.sum(x, axis=0, keepdims=True)
        den = jnp.maximum(den, TINY)
        acc = jnp.sum(jnp.dot((e_new / den).astype(BF16), hexp, preferred_element_type=F32) * vn, axis=0, keepdims=True)
        for x, v in zip(e, vals):
            p = jnp.dot((x / den).astype(BF16), hexp, preferred_element_type=F32)
            acc = acc + jnp.sum(p * v, axis=0, keepdims=True)
        outs.append(acc)
    o_ref[...] = jnp.concatenate(outs, axis=0)


def dilated_decode(z, cache_k, cache_v):
    B, Tc = cache_k.shape[:2]
    Q = z.shape[0] // B
    W = C_WIDTH
    (w1, d1), (w4, d4), (w16, d16) = DIL_PATTERNS
    assert d1 == 1 and all(w // d == DIL_TQ and Tc % (d * DIL_TQ) == 0 and (d == 1 or Q <= d) for w, d in DIL_PATTERNS)
    assert Q <= DIL_TQ and len(DIL_PATTERNS) == 3
    NR = 16
    zv = jnp.pad(z.reshape(B, Q, 3 * W), ((0, 0), (0, NR - Q), (0, 0)))
    new = lambda c: pl.BlockSpec((None, NR, W), lambda b: (b, 0, c))
    view = lambda a, d: a.reshape(B, Tc // d, d * W)
    last = lambda d: pl.BlockSpec((None, DIL_TQ, min(d, Q) * W), lambda b: (b, Tc // d // DIL_TQ - 1, 0))
    head = jnp.arange(W)[:, None] // HEAD_DIM == jnp.arange(LANES)[None, :]
    out = pl.pallas_call(
        _dil_decode_body,
        grid=(B,),
        in_specs=[new(0), new(1), new(2), last(d1), last(d4), last(d16), last(d1), last(d4), last(d16),
                  pl.BlockSpec((W, LANES), lambda b: (0, 0)), pl.BlockSpec((LANES, W), lambda b: (0, 0))],
        out_specs=pl.BlockSpec((None, Q, W), lambda b: (b, 0, 0)),
        out_shape=jax.ShapeDtypeStruct((B, Q, W), F32),
        compiler_params=_cparams("parallel"),
        name="dilated_decode",
    )(zv, zv, zv, view(cache_k, d1), view(cache_k, d4), view(cache_k, d16),
      view(cache_v, d1), view(cache_v, d4), view(cache_v, d16), head.astype(BF16), head.T.astype(BF16))
    return out.reshape(B * Q, W)


DEC_FC = 256
DEC_NR = 8


def _dil_window_body(q_ref, kt_ref, vt_ref, knt_ref, vnt_ref, ko_ref, vo_ref, att_ref, *, nq):
    FC, Tc = kt_ref.shape
    NR = q_ref.shape[0]
    nh = FC // HEAD_DIM
    kt, vt, knt, vnt = kt_ref[...], vt_ref[...], knt_ref[...], vnt_ref[...]
    keep = lax.broadcasted_iota(jnp.int32, (FC, LANES), 1) < LANES - nq
    for src, new, out in ((kt, knt, ko_ref), (vt, vnt, vo_ref)):
        r = pltpu.roll(src, Tc - nq, 1)
        out[:, :Tc - LANES] = r[:, :Tc - LANES]
        out[:, Tc - LANES:] = jnp.where(keep, r[:, Tc - LANES:], new)

    q = q_ref[...] * (HEAD_DIM ** -0.5)
    fhead = lax.broadcasted_iota(jnp.int32, (NR, FC), 1) // HEAD_DIM
    qbd = jnp.concatenate([jnp.where(fhead == h, q, 0.0) for h in range(nh)], axis=0).astype(BF16)
    s = jnp.dot(qbd, jnp.concatenate([kt, knt], axis=1).astype(BF16), preferred_element_type=F32)
    R, C = s.shape
    col = lax.broadcasted_iota(jnp.int32, (R, C), 1)
    pos = jnp.where(col < Tc, col, col - (LANES - nq))
    real = jnp.logical_or(col < Tc, col >= Tc + LANES - nq)
    dist = Tc + (lax.broadcasted_iota(jnp.int32, (R, 1), 0) & (NR - 1)) - pos
    cnt = jnp.zeros((R, C), F32)
    for w, d in DIL_PATTERNS:
        hit = jnp.logical_and(jnp.logical_and(dist >= 0, dist <= w), (dist & (d - 1)) == 0)
        cnt = cnt + jnp.where(jnp.logical_and(hit, real), 1.0, 0.0)
    s = jnp.where(cnt > 0.0, s, NEG)
    e = cnt * jnp.exp(s - jnp.max(s, axis=-1, keepdims=True))
    p = e / jnp.maximum(jnp.sum(e, axis=-1, keepdims=True), TINY)
    o = lax.dot_general(p.astype(BF16), jnp.concatenate([vt, vnt], axis=1).astype(BF16), (((1,), (1,)), ((), ())),
                        preferred_element_type=F32)
    att = jnp.where(fhead == 0, o[0:NR], 0.0)
    for h in range(1, nh):
        att = att + jnp.where(fhead == h, o[h * NR:(h + 1) * NR], 0.0)
    att_ref[...] = att


def dilated_window_step(z, cache_k, cache_v):
    B, Tc = cache_k.shape[:2]
    Q = z.shape[0] // B
    W = C_WIDTH
    assert all(d & (d - 1) == 0 and w <= Tc for w, d in DIL_PATTERNS) and Q <= DEC_NR and Tc % LANES == 0
    assert W % DEC_FC == 0 and DEC_FC % HEAD_DIM == 0
    to_ft = lambda c: c.transpose(0, 2, 3, 1).reshape(B, W, Tc)
    from_ft = lambda c: c.reshape(B, C_HEADS, HEAD_DIM, Tc).transpose(0, 3, 1, 2)
    z4 = z.reshape(B, Q, 3, W)
    q = jnp.pad(z4[:, :, 0], ((0, 0), (0, DEC_NR - Q), (0, 0)))
    new_t = lambda c: jnp.pad(z4[:, :, c].transpose(0, 2, 1), ((0, 0), (0, 0), (LANES - Q, 0)))
    feat = lambda lanes: pl.BlockSpec((None, DEC_FC, lanes), lambda b, f: (b, f, 0))
    qspec = pl.BlockSpec((None, DEC_NR, DEC_FC), lambda b, f: (b, 0, f))
    ko, vo, att = pl.pallas_call(
        functools.partial(_dil_window_body, nq=Q),
        grid=(B, W // DEC_FC),
        in_specs=[qspec, feat(Tc), feat(Tc), feat(LANES), feat(LANES)],
        out_specs=[feat(Tc), feat(Tc), qspec],
        out_shape=[jax.ShapeDtypeStruct((B, W, Tc), F32), jax.ShapeDtypeStruct((B, W, Tc), F32),
                   jax.ShapeDtypeStruct((B, DEC_NR, W), F32)],
        compiler_params=_cparams("parallel", "parallel"),
        name="dilated_window_step",
    )(q, to_ft(cache_k), to_ft(cache_v), new_t(1), new_t(2))
    return att[:, :Q].reshape(B * Q, W), from_ft(ko), from_ft(vo)


def odd_mixer_sample(h, p0, past, P, o):
    buf_k, buf_v = past
    B, T, D = h.shape
    assert buf_k.shape[1] == C_WINDOW and p0 >= C_WINDOW
    cos, sin = _rope_tables(jnp.tile(p0 + jnp.arange(T), B))
    z = inproj_odd(h.reshape(B * T, D), P['w_in_odd'][o], P['dil_q_gain'][o], P['dil_k_gain'][o], cos, sin)
    att, k_all, v_all = dilated_window_step(z, buf_k, buf_v)
    mix = matmul(att, P['w_out_odd'][o])
    return mix.reshape(B, T, D), (k_all, v_all)


def rms_norm(x, g):
    xf = x.astype(jnp.float32)
    y = xf * lax.rsqrt(jnp.mean(xf * xf, axis=-1, keepdims=True) + EPS)
    return (y * g.astype(jnp.float32)).astype(x.dtype)


def rope(x, pos):
    half = HEAD_DIM // 2
    freq = ROPE_THETA ** (-jnp.arange(half, dtype=jnp.float32) / half)
    ang = pos.astype(jnp.float32)[:, None] * freq[None, :]
    cos = jnp.cos(ang)[None, :, None, :]
    sin = jnp.sin(ang)[None, :, None, :]
    xf = x.astype(jnp.float32)
    x1, x2 = xf[..., :half], xf[..., half:]
    return jnp.concatenate([x1 * cos - x2 * sin, x2 * cos + x1 * sin], axis=-1).astype(x.dtype)


def masked_softmax(s, mask):
    s = jnp.where(mask, s.astype(jnp.float32), NEG)
    m = jnp.max(s, axis=-1, keepdims=True)
    e = jnp.where(mask, jnp.exp(s - m), 0.0)
    den = jnp.maximum(jnp.sum(e, axis=-1, keepdims=True), TINY)
    return e / den, m + jnp.log(den)


def split_cols(z, sizes):
    out, o = [], 0
    for s in sizes:
        out.append(z[..., o:o + s])
        o += s
    return out


def over_query_blocks(fn, T):
    if T > Q_BLOCK and T % Q_BLOCK == 0:
        out = lax.map(lambda i: fn(i * Q_BLOCK, Q_BLOCK), jnp.arange(T // Q_BLOCK))
        return jnp.swapaxes(out, 0, 1).reshape(out.shape[1], T, out.shape[-1])
    return fn(0, T)


def gather_pages(pool, page_table):
    g = pool[page_table]
    return g.reshape(g.shape[0], g.shape[1] * g.shape[2], g.shape[3], g.shape[4])


def causal_conv(u, buf, w, b):
    T = u.shape[1]
    up = jnp.concatenate([buf, u], axis=1)
    y = b + up[:, 0:T] * w[0]
    for j in range(1, CONV_W):
        y = y + up[:, j:j + T] * w[j]
    return y, up[:, T:]


def _lin_comb(l, r):
    return (l[0] * r[0], r[0] * l[1] + r[1])


def rg_lru(u, h0, wa, ba, wi, bi, lam):
    B, T, W = u.shape
    ub = u.reshape(B, T, LRU_BLOCKS, LRU_BW)
    r = jax.nn.sigmoid((jnp.einsum('btki,kij->btkj', ub, wa).reshape(B, T, W) + ba).astype(jnp.float32))
    gi = jax.nn.sigmoid((jnp.einsum('btki,kij->btkj', ub, wi).reshape(B, T, W) + bi).astype(jnp.float32))
    log_a = -LRU_C * r * jax.nn.softplus(-lam.astype(jnp.float32))
    a = jnp.exp(log_a)
    b = jnp.sqrt(-jnp.expm1(2.0 * log_a)) * (gi * u.astype(jnp.float32))
    b = b.at[:, 0].add(a[:, 0] * h0.astype(jnp.float32))
    _, h = lax.associative_scan(_lin_comb, (a, b), axis=1)
    return h.astype(u.dtype), h[:, -1].astype(u.dtype)


def compress(rows, w, t_pad):
    B, T = rows.shape[:2]
    r = jnp.pad(rows, ((0, 0), (0, t_pad - T), (0, 0), (0, 0)))
    r = r.reshape(B, t_pad // CMP_LEN, CMP_LEN, NSA_KV, HEAD_DIM)
    return jnp.einsum('bnjgd,jde->bnge', r, w)


def sel_blocks(rows, t_pad):
    B, T = rows.shape[:2]
    r = jnp.pad(rows, ((0, 0), (0, t_pad - T), (0, 0), (0, 0)))
    return r.reshape(B, t_pad // SEL_LEN, SEL_LEN, NSA_KV, HEAD_DIM).transpose(0, 3, 1, 2, 4)


def nsa_attend(q, q_rot, gates, qpos, wstart, kc, vc, ksg, vsg, kw_p, vw_p):
    B, Q = q.shape[:2]
    G, M = NSA_KV, NSA_GROUP
    scale = HEAD_DIM ** -0.5
    qg = q.reshape(B, Q, G, M, HEAD_DIM)
    qr = q_rot.reshape(B, Q, G, M, HEAD_DIM)
    ncb = kc.shape[1]
    s_c = jnp.einsum('bqgmd,bngd->bgmqn', qg, kc) * scale
    cmask = ((jnp.arange(ncb) + 1) * CMP_LEN - 1)[None, :] <= qpos[:, None]
    p_c, _ = masked_softmax(s_c, cmask)
    o_c = jnp.einsum('bgmqn,bngd->bqgmd', p_c.astype(vc.dtype), vc)
    nsb = ksg.shape[2]
    imp = p_c.sum(axis=2).reshape(B, G, Q, nsb, CMP_PER_SEL).sum(-1)
    sb = jnp.arange(nsb)[None, :]
    cur = (qpos // SEL_LEN)[:, None]
    valid = sb * SEL_LEN <= qpos[:, None]
    forced = (sb == 0) | (sb == cur) | (sb == cur - 1)
    score = jnp.where(valid, imp + jnp.where(forced, FORCE, 0.0), NEG)
    top_s, top_i = lax.top_k(score, min(N_SEL, nsb))
    n = top_i.shape[-1]
    bi = jnp.arange(B)[:, None, None, None]
    gi = jnp.arange(G)[None, :, None, None]
    k_sel = ksg[bi, gi, top_i]
    v_sel = vsg[bi, gi, top_i]
    kpos = top_i[..., None] * SEL_LEN + jnp.arange(SEL_LEN)
    smask = (top_s > 0.5 * NEG)[..., None] & (kpos <= qpos[None, None, :, None, None])
    s_s = jnp.einsum('bqgmd,bgqnld->bgmqnl', qr, k_sel).reshape(B, G, M, Q, n * SEL_LEN) * scale
    p_s, _ = masked_softmax(s_s, smask.reshape(B, G, 1, Q, n * SEL_LEN))
    o_s = jnp.einsum('bgmqk,bgqkd->bqgmd', p_s.astype(v_sel.dtype), v_sel.reshape(B, G, Q, n * SEL_LEN, HEAD_DIM))
    span = SW_WIN + Q
    kwb = lax.dynamic_slice_in_dim(kw_p, wstart, span, axis=1)
    vwb = lax.dynamic_slice_in_dim(vw_p, wstart, span, axis=1)
    qloc = wstart + jnp.arange(Q)
    kloc = wstart - SW_WIN + jnp.arange(span)
    dist = qloc[:, None] - kloc[None, :]
    wmask = (dist >= 0) & (dist <= SW_WIN) & (kloc[None, :] >= 0)
    s_w = jnp.einsum('bqgmd,bkgd->bgmqk', qr, kwb) * scale
    p_w, _ = masked_softmax(s_w, wmask)
    o_w = jnp.einsum('bgmqk,bkgd->bqgmd', p_w.astype(vwb.dtype), vwb)
    gt = gates.reshape(B, Q, G, M, 3).astype(o_c.dtype)
    o = gt[..., 0:1] * o_c + gt[..., 1:2] * o_s + gt[..., 2:3] * o_w
    return o.reshape(B, Q, NSA_HEADS * HEAD_DIM)


def even_mixer(h, p0, past, P, e):
    lru_h0, conv_buf, past_ck, past_cv, past_sk, past_sv, win_k, win_v = past
    B, T, _ = h.shape
    z = matmul(h, P['w_in_even'][e])
    u, y, q, ck, cv, sk, sv, wk, wv, g = split_cols(z, EVEN_SIZES)
    uc, conv_new = causal_conv(u, conv_buf, P['conv_w'][e], P['conv_b'][e])
    hl, h_last = rg_lru(uc, lru_h0, P['lru_wa'][e], P['lru_ba'][e], P['lru_wi'][e], P['lru_bi'][e], P['lru_lambda'][e])
    lru_out = hl * jax.nn.gelu(y)
    pos = p0 + jnp.arange(T)
    kg = P['nsa_k_gain'][e]
    kvs = (B, T, NSA_KV, HEAD_DIM)
    q = rms_norm(q.reshape(B, T, NSA_HEADS, HEAD_DIM), P['nsa_q_gain'][e])
    q_rot = rope(q, pos)
    ck = rms_norm(ck.reshape(kvs), kg[0])
    cv = cv.reshape(kvs)
    sk = rope(rms_norm(sk.reshape(kvs), kg[1]), pos)
    sv = sv.reshape(kvs)
    wk = rope(rms_norm(wk.reshape(kvs), kg[2]), pos)
    wv = wv.reshape(kvs)
    gates = jax.nn.sigmoid(g.astype(jnp.float32)).reshape(B, T, NSA_HEADS, 3)
    ck_all = jnp.concatenate([past_ck, ck], axis=1)
    cv_all = jnp.concatenate([past_cv, cv], axis=1)
    sk_all = jnp.concatenate([past_sk, sk], axis=1)
    sv_all = jnp.concatenate([past_sv, sv], axis=1)
    t_k = ck_all.shape[1]
    t_pad = -(-t_k // SEL_LEN) * SEL_LEN
    kc = compress(ck_all, P['w_cmp_k'][e], t_pad)
    vc = compress(cv_all, P['w_cmp_v'][e], t_pad)
    ksg = sel_blocks(sk_all, t_pad)
    vsg = sel_blocks(sv_all, t_pad)
    wk_all = jnp.concatenate([win_k, wk], axis=1)
    wv_all = jnp.concatenate([win_v, wv], axis=1)
    wb = win_k.shape[1]
    pad_w = ((0, 0), (SW_WIN, 0), (0, 0), (0, 0))
    wk_p = jnp.pad(wk_all, pad_w)
    wv_p = jnp.pad(wv_all, pad_w)

    def block(q0, qb):
        sl = lambda a: lax.dynamic_slice_in_dim(a, q0, qb, axis=1)
        qpos = p0 + q0 + jnp.arange(qb)
        return nsa_attend(sl(q), sl(q_rot), sl(gates), qpos, wb + q0, kc, vc, ksg, vsg, wk_p, wv_p)

    nsa_out = over_query_blocks(block, T)
    mix = matmul(jnp.concatenate([lru_out, nsa_out.astype(lru_out.dtype)], axis=-1), P['w_out_even'][e])
    keep = min(SW_WIN, wk_all.shape[1])
    return mix, (h_last, conv_new, ck, cv, sk, sv, wk_all[:, -keep:], wv_all[:, -keep:])


def dilated_attend(q, kp, vp, qloc):
    scale = HEAD_DIM ** -0.5
    outs, lses = [], []
    for w, d in DIL_PATTERNS:
        kl = qloc[:, None] - d * jnp.arange(w // d + 1)[None, :]
        kg = kp[:, kl + C_WINDOW]
        vg = vp[:, kl + C_WINDOW]
        s = jnp.einsum('bqhd,bqnhd->bhqn', q, kg) * scale
        p, lse = masked_softmax(s, kl >= 0)
        outs.append(jnp.einsum('bhqn,bqnhd->bqhd', p.astype(vg.dtype), vg))
        lses.append(lse)
    wts = jax.nn.softmax(jnp.stack(lses), axis=0)
    o = jnp.swapaxes(wts[0], 1, 2).astype(outs[0].dtype) * outs[0]
    for i in range(1, len(DIL_PATTERNS)):
        o = o + jnp.swapaxes(wts[i], 1, 2).astype(outs[i].dtype) * outs[i]
    return o.reshape(o.shape[0], o.shape[1], C_WIDTH)


def odd_mixer(h, p0, past, P, o):
    buf_k, buf_v = past
    B, T, _ = h.shape
    z = matmul(h, P['w_in_odd'][o])
    q, k, v = split_cols(z, (C_WIDTH,) * 3)
    shp = (B, T, C_HEADS, HEAD_DIM)
    pos = p0 + jnp.arange(T)
    q = rope(rms_norm(q.reshape(shp), P['dil_q_gain'][o]), pos)
    k = rope(rms_norm(k.reshape(shp), P['dil_k_gain'][o]), pos)
    v = v.reshape(shp)
    k_all = jnp.concatenate([buf_k, k], axis=1)
    v_all = jnp.concatenate([buf_v, v], axis=1)
    wb = buf_k.shape[1]
    pad = ((0, 0), (C_WINDOW, 0), (0, 0), (0, 0))
    kp = jnp.pad(k_all, pad)
    vp = jnp.pad(v_all, pad)

    def block(q0, qb):
        return dilated_attend(lax.dynamic_slice_in_dim(q, q0, qb, axis=1), kp, vp, wb + q0 + jnp.arange(qb))

    att = over_query_blocks(block, T)
    keep = min(C_WINDOW, k_all.shape[1])
    return matmul(att, P['w_out_odd'][o]), (k_all[:, -keep:], v_all[:, -keep:])


def _expert_ffn_body(blk_e_ref, n_used_ref, x_ref, wg_ref, wu_ref, wd_ref, o_ref):
    @pl.when(pl.program_id(0) < n_used_ref[0])
    def _():
        x = x_ref[...].astype(BF16)
        g = jnp.dot(x, wg_ref[...].astype(BF16), preferred_element_type=F32)
        u = jnp.dot(x, wu_ref[...].astype(BF16), preferred_element_type=F32)
        hid = (g * jax.nn.sigmoid(g)) * u
        o_ref[...] = jnp.dot(hid.astype(BF16), wd_ref[...].astype(BF16), preferred_element_type=F32)

    @pl.when(pl.program_id(0) >= n_used_ref[0])
    def _():
        o_ref[...] = jnp.zeros_like(o_ref)


def expert_ffn(xg, blk_e, n_used, w_gate, w_up, w_down, rows):
    n_slots, D = xg.shape
    DE = w_gate.shape[-1]
    wspec = lambda a, b: pl.BlockSpec((None, a, b), lambda j, be, nu: (be[j], 0, 0))
    return pl.pallas_call(
        _expert_ffn_body,
        grid_spec=pltpu.PrefetchScalarGridSpec(
            num_scalar_prefetch=2,
            grid=(n_slots // rows,),
            in_specs=[pl.BlockSpec((rows, D), lambda j, be, nu: (j, 0)), wspec(D, DE), wspec(D, DE), wspec(DE, D)],
            out_specs=pl.BlockSpec((rows, D), lambda j, be, nu: (j, 0))),
        out_shape=jax.ShapeDtypeStruct((n_slots, D), F32),
        compiler_params=_cparams("arbitrary"),
        name="expert_ffn",
    )(blk_e, n_used, xg, w_gate, w_up, w_down)


def routed_experts(xf, eid, wts, w_gate, w_up, w_down):
    N, D = xf.shape
    M = N * TOP_K
    rows = 256 if M >= 256 * N_EXPERTS else MOE_BLK
    e_flat = eid.reshape(M)
    order = jnp.argsort(e_flat)
    e_sorted = e_flat[order]
    counts = jnp.bincount(e_flat, length=N_EXPERTS)
    padded = (counts + rows - 1) // rows * rows
    start = jnp.cumsum(counts) - counts
    pend = jnp.cumsum(padded)
    pstart = pend - padded
    dest = (pstart[e_sorted] + (jnp.arange(M) - start[e_sorted])).astype(jnp.int32)
    n_blocks = -(-(M + N_EXPERTS * (rows - 1)) // rows)
    n_slots = n_blocks * rows
    slot_tok = jnp.zeros((n_slots,), jnp.int32).at[dest].set((order // TOP_K).astype(jnp.int32))
    blk_e = jnp.minimum(jnp.searchsorted(pend, jnp.arange(n_blocks) * rows, side='right'), N_EXPERTS - 1)
    n_used = (pend[-1:] // rows).astype(jnp.int32)
    yb = expert_ffn(xf[slot_tok], blk_e.astype(jnp.int32), n_used, w_gate, w_up, w_down, rows)
    pos = jnp.zeros((M,), jnp.int32).at[order].set(dest).reshape(N, TOP_K)
    y = yb[pos[:, 0]] * wts[:, 0:1].astype(xf.dtype)
    for k in range(1, TOP_K):
        y = y + yb[pos[:, k]] * wts[:, k:k + 1].astype(xf.dtype)
    return y


def hier_moe(x, P, layer):
    B, T, D = x.shape
    N = B * T
    xf = x.reshape(N, D)
    rows = jnp.arange(N)
    w_router = jnp.concatenate([P['w_router_group'][layer], P['w_router_exp'][layer]], axis=1)
    logits = matmul(xf, jnp.pad(w_router, ((0, 0), (0, LANES - N_GROUPS - N_EXPERTS))))
    g_logit = (logits[:, :N_GROUPS] + P['b_router_group'][layer]).astype(jnp.float32)
    g_top = jnp.argmax(g_logit, axis=-1)
    g_w = jax.nn.softmax(g_logit, axis=-1)[rows, g_top]
    e_logit = (logits[:, N_GROUPS:N_GROUPS + N_EXPERTS] + P['b_router_exp'][layer]).astype(jnp.float32)
    e_logit = e_logit.reshape(N, N_GROUPS, EXP_PER_GROUP)[rows, g_top]
    top_l, top_i = lax.top_k(e_logit, TOP_K)
    top_p = jax.nn.softmax(top_l, axis=-1)
    eid = (g_top[:, None] * EXP_PER_GROUP + top_i).astype(jnp.int32)
    wts = g_w[:, None] * top_p
    y = routed_experts(xf, eid, wts, P['w_exp_gate'][layer], P['w_exp_up'][layer], P['w_exp_down'][layer])
    return y.reshape(B, T, D)


def trunk(x, p0, even_past, odd_past, P):
    even_new, odd_new = [], []
    for layer in range(DEPTH):
        hn = rms_norm(x, P['norm_mix'][layer])
        if layer % 2 == 0:
            if even_past is None:
                mix, st = even_mixer_prompt(hn, P, layer // 2)
            else:
                mix, st = even_mixer(hn, p0, even_past[layer // 2], P, layer // 2)
            even_new.append(st)
        else:
            if odd_past is None:
                mix, st = odd_mixer_prompt(hn, P, layer // 2)
            else:
                mix, st = odd_mixer_sample(hn, p0, odd_past[layer // 2], P, layer // 2)
            odd_new.append(st)
        x = x + mix.astype(x.dtype)
        x = x + hier_moe(rms_norm(x, P['norm_ffn'][layer]), P, layer).astype(x.dtype)
    return x, even_new, odd_new


def stack_layers(states):
    return [jnp.stack(items) for items in zip(*states)]


def kernel(x_prompt, x_sample, state_lru_h, state_lru_conv, cache_cmp_k, cache_cmp_v, cache_sel_k, cache_sel_v, cache_win_k, cache_win_v, cache_dil_k, cache_dil_v, page_table, norm_mix, norm_ffn, w_in_even, conv_w, conv_b, lru_wa, lru_ba, lru_wi, lru_bi, lru_lambda, nsa_q_gain, nsa_k_gain, w_cmp_k, w_cmp_v, w_out_even, w_in_odd, dil_q_gain, dil_k_gain, w_out_odd, w_router_group, b_router_group, w_router_exp, b_router_exp, w_exp_gate, w_exp_up, w_exp_down):
    P = dict(norm_mix=norm_mix, norm_ffn=norm_ffn, w_in_even=w_in_even, conv_w=conv_w, conv_b=conv_b,
             lru_wa=lru_wa, lru_ba=lru_ba, lru_wi=lru_wi, lru_bi=lru_bi, lru_lambda=lru_lambda,
             nsa_q_gain=nsa_q_gain, nsa_k_gain=nsa_k_gain, w_cmp_k=w_cmp_k, w_cmp_v=w_cmp_v,
             w_out_even=w_out_even, w_in_odd=w_in_odd, dil_q_gain=dil_q_gain, dil_k_gain=dil_k_gain,
             w_out_odd=w_out_odd, w_router_group=w_router_group, b_router_group=b_router_group,
             w_router_exp=w_router_exp, b_router_exp=b_router_exp, w_exp_gate=w_exp_gate,
             w_exp_up=w_exp_up, w_exp_down=w_exp_down)
    y_prompt, ev_p, od_p = trunk(x_prompt, 0, None, None, P)
    p0 = page_table.shape[1] * PAGE_SIZE
    ev1 = [(state_lru_h[e], state_lru_conv[e],
            gather_pages(cache_cmp_k[e], page_table), gather_pages(cache_cmp_v[e], page_table),
            gather_pages(cache_sel_k[e], page_table), gather_pages(cache_sel_v[e], page_table),
            cache_win_k[e], cache_win_v[e]) for e in range(N_EVEN)]
    od1 = [(cache_dil_k[o], cache_dil_v[o]) for o in range(N_ODD)]
    y_sample, ev_s, od_s = trunk(x_sample, p0, ev1, od1, P)
    p_lru_h, p_lru_conv, p_cmp_k, p_cmp_v, p_sel_k, p_sel_v, p_win_k, p_win_v = stack_layers(ev_p)
    p_dil_k, p_dil_v = stack_layers(od_p)
    s_lru_h, s_lru_conv, s_cmp_k, s_cmp_v, s_sel_k, s_sel_v, s_win_k, s_win_v = stack_layers(ev_s)
    s_dil_k, s_dil_v = stack_layers(od_s)
    return (y_prompt, y_sample, p_lru_h, p_lru_conv, p_cmp_k, p_cmp_v, p_sel_k, p_sel_v, p_win_k, p_win_v, p_dil_k, p_dil_v, s_lru_h, s_lru_conv, s_cmp_k, s_cmp_v, s_sel_k, s_sel_v, s_win_k, s_win_v, s_dil_k, s_dil_v)
```

```python
import functools

import jax, jax.numpy as jnp
from jax import lax
import numpy as np
from jax.experimental import pallas as pl
from jax.experimental.pallas import tpu as pltpu

D_MODEL = 1024
BATCH = 2
SEQ = 8192
DEPTH = 2
DEC_BATCH = 128
DEC_SEQ = 4
PAST_LEN = 2048
PAGE_SIZE = 128

HEAD_DIM = 64
LRU_WIDTH = D_MODEL // 2
LRU_BLOCKS = LRU_WIDTH // HEAD_DIM
LRU_BW = LRU_WIDTH // LRU_BLOCKS
CONV_W = 4
LRU_C = 8.0
NSA_HEADS = (D_MODEL // 2) // HEAD_DIM
NSA_KV = 2
NSA_GROUP = NSA_HEADS // NSA_KV
CMP_LEN = 32
SEL_LEN = 64
CMP_PER_SEL = SEL_LEN // CMP_LEN
N_SEL = 16
SW_WIN = 512
FORCE = 1000.0
C_HEADS = D_MODEL // HEAD_DIM
DIL_PATTERNS = ((128, 1), (512, 4), (2048, 16))
C_WINDOW = 2048
N_GROUPS = 4
EXP_PER_GROUP = 8
N_EXPERTS = N_GROUPS * EXP_PER_GROUP
D_EXPERT = D_MODEL // 2
TOP_K = 2
MOE_BLK = 128
Q_BLOCK = 128
ROPE_THETA = 10000.0
EPS = 1e-6
NEG = -1e30
TINY = 1e-30
N_EVEN = (DEPTH + 1) // 2
N_ODD = DEPTH // 2
EVEN_SIZES = (LRU_WIDTH, LRU_WIDTH, NSA_HEADS * HEAD_DIM) + (NSA_KV * HEAD_DIM,) * 6 + (3 * NSA_HEADS,)
EVEN_COLS = sum(EVEN_SIZES)
EVEN_OUT = LRU_WIDTH + NSA_HEADS * HEAD_DIM
C_WIDTH = C_HEADS * HEAD_DIM


def _mm_body(x_ref, w_ref, o_ref):
    o_ref[...] = jnp.dot(x_ref[...].astype(jnp.bfloat16), w_ref[...].astype(jnp.bfloat16),
                         preferred_element_type=jnp.float32)


def matmul(x, w, tm=512, tn=512):
    lead = x.shape[:-1]
    K = x.shape[-1]
    M = w.shape[-1]
    x2 = x.reshape(-1, K)
    N = x2.shape[0]
    tm = min(tm, N)
    tn = min(tn, M)
    out = pl.pallas_call(
        _mm_body,
        grid=(pl.cdiv(N, tm), pl.cdiv(M, tn)),
        in_specs=[pl.BlockSpec((tm, K), lambda i, j: (i, 0)),
                  pl.BlockSpec((K, tn), lambda i, j: (0, j))],
        out_specs=pl.BlockSpec((tm, tn), lambda i, j: (i, j)),
        out_shape=jax.ShapeDtypeStruct((N, M), jnp.float32),
        name="matmul",
    )(x2, w)
    return out.reshape(lead + (M,))


LANES = 128
VMEM_LIMIT = 48 * 1024 * 1024
BF16 = jnp.bfloat16
F32 = jnp.float32


def _cparams(*sem):
    return pltpu.CompilerParams(dimension_semantics=sem, vmem_limit_bytes=VMEM_LIMIT)


def _rope_tables(pos):
    half = HEAD_DIM // 2
    freq = ROPE_THETA ** (-jnp.arange(half, dtype=F32) / half)
    ang = pos.astype(F32)[:, None] * freq[None, :]
    cos, sin = jnp.cos(ang), jnp.sin(ang)
    return jnp.tile(cos, (1, 4)), jnp.tile(jnp.concatenate([-sin, sin], axis=1), (1, 2))


def _head_mean_matrix(width):
    h = jnp.arange(width) // HEAD_DIM
    return jnp.where(h[:, None] == h[None, :], 1.0 / HEAD_DIM, 0.0).astype(BF16)


def _head_norm(x, gain, gmat):
    x2 = x * x
    hi = x2.astype(BF16)
    lo = (x2 - hi.astype(F32)).astype(BF16)
    ms = jnp.dot(hi, gmat, preferred_element_type=F32) + jnp.dot(lo, gmat, preferred_element_type=F32)
    return x * lax.rsqrt(ms + EPS) * gain


def _rope_lanes(y, cos, sin):
    hi_half = (lax.broadcasted_iota(jnp.int32, (y.shape[0], LANES), 1) & (HEAD_DIM // 2)) != 0
    outs = []
    for c in range(y.shape[1] // LANES):
        yc = y[:, c * LANES:(c + 1) * LANES]
        partner = jnp.where(hi_half, pltpu.roll(yc, HEAD_DIM // 2, 1), pltpu.roll(yc, LANES - HEAD_DIM // 2, 1))
        outs.append(yc * cos + partner * sin)
    return outs[0] if len(outs) == 1 else jnp.concatenate(outs, axis=1)


def _inproj_odd_body(x_ref, w_ref, cos_ref, sin_ref, gain_ref, gmat_ref, o_ref, *, n_rope):
    j = pl.program_id(1)
    acc = jnp.dot(x_ref[...].astype(BF16), w_ref[...].astype(BF16), preferred_element_type=F32)

    @pl.when(j < n_rope)
    def _():
        o_ref[...] = _rope_lanes(_head_norm(acc, gain_ref[0], gmat_ref[...]), cos_ref[...], sin_ref[...])

    @pl.when(j >= n_rope)
    def _():
        o_ref[...] = acc


def inproj_odd(h2, w, q_gain, k_gain, cos, sin, tm=512, tn=512):
    N, K = h2.shape
    M = w.shape[-1]
    tm = min(tm, N)
    reps = tn // HEAD_DIM
    per = C_WIDTH // tn
    gains = jnp.concatenate([jnp.tile(jnp.tile(q_gain, reps)[None], (per, 1)),
                             jnp.tile(jnp.tile(k_gain, reps)[None], (per, 1)),
                             jnp.ones((per, tn), F32)], axis=0)[:, None, :]
    return pl.pallas_call(
        functools.partial(_inproj_odd_body, n_rope=2 * per),
        grid=(N // tm, M // tn),
        in_specs=[pl.BlockSpec((tm, K), lambda i, j: (i, 0)),
                  pl.BlockSpec((K, tn), lambda i, j: (0, j)),
                  pl.BlockSpec((tm, LANES), lambda i, j: (i, 0)),
                  pl.BlockSpec((tm, LANES), lambda i, j: (i, 0)),
                  pl.BlockSpec((1, 1, tn), lambda i, j: (j, 0, 0)),
                  pl.BlockSpec((tn, tn), lambda i, j: (0, 0))],
        out_specs=pl.BlockSpec((tm, tn), lambda i, j: (i, j)),
        out_shape=jax.ShapeDtypeStruct((N, M), F32),
        compiler_params=_cparams("parallel", "arbitrary"),
        name="inproj_odd",
    )(h2, w, cos, sin, gains, _head_mean_matrix(tn))


DIL_TQ = 128


def _dil_body(q_ref, kc_ref, kp_ref, vc_ref, vp_ref, o_ref, lse_ref):
    i = pl.program_id(2)
    tq = q_ref.shape[0]
    u = lax.broadcasted_iota(jnp.int32, (2 * tq, 2 * tq), 0) & (tq - 1)
    c = lax.broadcasted_iota(jnp.int32, (2 * tq, 2 * tq), 1)
    in_prev = jnp.logical_and(jnp.logical_and(c < tq, c >= u), i > 0)
    mask = jnp.logical_or(in_prev, jnp.logical_and(c >= tq, c - tq <= u))
    low = lax.broadcasted_iota(jnp.int32, (tq, LANES), 1) < HEAD_DIM
    scale = HEAD_DIM ** -0.5
    dn = (((1,), (1,)), ((), ()))
    for hp in range(q_ref.shape[1] // LANES):
        sl = slice(hp * LANES, (hp + 1) * LANES)
        q2 = q_ref[:, sl] * scale
        qs = jnp.concatenate([jnp.where(low, q2, 0.0), jnp.where(low, 0.0, q2)], axis=0).astype(BF16)
        kk = jnp.concatenate([kp_ref[:, sl], kc_ref[:, sl]], axis=0).astype(BF16)
        vv = jnp.concatenate([vp_ref[:, sl], vc_ref[:, sl]], axis=0).astype(BF16)
        s = jnp.where(mask, lax.dot_general(qs, kk, dn, preferred_element_type=F32), NEG)
        m = jnp.max(s, axis=-1, keepdims=True)
        e = jnp.exp(s - m)
        den = jnp.maximum(jnp.sum(e, axis=-1, keepdims=True), TINY)
        o = jnp.dot(e.astype(BF16), vv, preferred_element_type=F32) / den
        lse = jnp.broadcast_to(m + jnp.log(den), (2 * tq, LANES))
        o_ref[:, sl] = jnp.where(low, o[:tq], o[tq:])
        lse_ref[:, sl] = jnp.where(low, lse[:tq], lse[tq:])


def dilated_pattern(z, B, T, d):
    W = C_WIDTH
    Td = T // d
    zv = z.reshape(B, Td, d * 3 * W)
    blk = lambda off, prev: pl.BlockSpec(
        (None, DIL_TQ, W), (lambda b, r, i: (b, jnp.maximum(i - 1, 0), 3 * r + off)) if prev
        else (lambda b, r, i: (b, i, 3 * r + off)))
    o_spec = pl.BlockSpec((None, DIL_TQ, W), lambda b, r, i: (b, i, r))
    out, lse = pl.pallas_call(
        _dil_body,
        grid=(B, d, Td // DIL_TQ),
        in_specs=[blk(0, False), blk(1, False), blk(1, True), blk(2, False), blk(2, True)],
        out_specs=[o_spec, o_spec],
        out_shape=[jax.ShapeDtypeStruct((B, Td, d * W), F32)] * 2,
        compiler_params=_cparams("parallel", "parallel", "arbitrary"),
        name="dilated_d%d" % d,
    )(zv, zv, zv, zv, zv)
    return out.reshape(B * T, W), lse.reshape(B * T, W)


def _combine_outproj_body(o1, o2, o3, l1, l2, l3, w_ref, out_ref):
    m = jnp.maximum(jnp.maximum(l1[...], l2[...]), l3[...])
    e1, e2, e3 = jnp.exp(l1[...] - m), jnp.exp(l2[...] - m), jnp.exp(l3[...] - m)
    den = e1 + e2 + e3
    att = (e1 / den) * o1[...] + (e2 / den) * o2[...] + (e3 / den) * o3[...]
    out_ref[...] = jnp.dot(att.astype(BF16), w_ref[...].astype(BF16), preferred_element_type=F32)


def combine_outproj(outs, lses, w, tm=256):
    N, W = outs[0].shape
    M = w.shape[-1]
    row = pl.BlockSpec((tm, W), lambda i: (i, 0))
    return pl.pallas_call(
        _combine_outproj_body,
        grid=(N // tm,),
        in_specs=[row] * 6 + [pl.BlockSpec((W, M), lambda i: (0, 0))],
        out_specs=pl.BlockSpec((tm, M), lambda i: (i, 0)),
        out_shape=jax.ShapeDtypeStruct((N, M), F32),
        compiler_params=_cparams("parallel"),
        name="dil_combine_outproj",
    )(*outs, *lses, w)


def odd_mixer_prompt(h, P, o):
    B, T, D = h.shape
    assert all(w // d == DIL_TQ and T % (d * DIL_TQ) == 0 for w, d in DIL_PATTERNS)
    cos, sin = _rope_tables(jnp.tile(jnp.arange(T), B))
    z = inproj_odd(h.reshape(B * T, D), P['w_in_odd'][o], P['dil_q_gain'][o], P['dil_k_gain'][o], cos, sin)
    res = [dilated_pattern(z, B, T, d) for _, d in DIL_PATTERNS]
    mix = combine_outproj([r[0] for r in res], [r[1] for r in res], P['w_out_odd'][o])
    keep = min(C_WINDOW, T)
    shp = (B, T, C_HEADS, HEAD_DIM)
    tail = z.reshape(B, T, 3 * C_WIDTH)[:, T - keep:]
    shp = (B, keep, C_HEADS, HEAD_DIM)
    return mix.reshape(B, T, D), (tail[..., C_WIDTH:2 * C_WIDTH].reshape(shp), tail[..., 2 * C_WIDTH:].reshape(shp))


EVEN_PAD = -(-EVEN_COLS // LANES) * LANES
NSA_TQ = 256
NSA_TK = 512
NSA_SPAN = SW_WIN + NSA_TQ
NSA_MAXBLK = LANES
SEL_OFF = -1e9
M_INIT = -3e38


def _dup_group(x, low):
    r = pltpu.roll(x, HEAD_DIM, 1)
    return jnp.where(low, x, r), jnp.where(low, r, x)


def _nsa_prep_body(zq_ref, zkv_ref, zg_ref, cos_ref, sin_ref, qg_ref, kg_ref, gmat_ref,
                   qn_ref, qr_ref, ck_ref, sk_ref, wk_ref, skd_ref, svd_ref, wkd_ref, wvd_ref, gt_ref):
    cos, sin = cos_ref[...], sin_ref[...]
    low = lax.broadcasted_iota(jnp.int32, (zq_ref.shape[0], LANES), 1) < HEAD_DIM
    qn = _head_norm(zq_ref[...], qg_ref[...], gmat_ref[...])
    qn_ref[...] = qn
    qr_ref[...] = _rope_lanes(qn, cos, sin)
    g1 = gmat_ref[0:LANES, 0:LANES]
    col = lambda c: zkv_ref[:, c * LANES:(c + 1) * LANES]
    ck_ref[...] = _head_norm(col(0), kg_ref[0:1, :], g1)
    sk = _rope_lanes(_head_norm(col(2), kg_ref[1:2, :], g1), cos, sin)
    wk = _rope_lanes(_head_norm(col(4), kg_ref[2:3, :], g1), cos, sin)
    sk_ref[...] = sk
    wk_ref[...] = wk
    for x, ref in ((sk, skd_ref), (col(3), svd_ref), (wk, wkd_ref), (col(5), wvd_ref)):
        d0, d1 = _dup_group(x, low)
        ref[0] = d0.astype(BF16)
        ref[1] = d1.astype(BF16)
    gt = 1.0 / (1.0 + jnp.exp(-zg_ref[...]))
    gt_ref[0] = gt
    gt_ref[1] = pltpu.roll(gt, LANES - 3 * NSA_GROUP, 1)


def nsa_prep(z, cos, sin, q_gain, k_gain, tm=256):
    N = z.shape[0]
    tm = min(tm, N)
    QW = NSA_HEADS * HEAD_DIM
    KW = NSA_KV * HEAD_DIM
    assert KW == LANES and QW % LANES == 0 and (LRU_WIDTH * 2) % QW == 0
    f = lambda shape: jax.ShapeDtypeStruct(shape, F32)
    b = lambda shape: jax.ShapeDtypeStruct(shape, BF16)
    row = lambda w: pl.BlockSpec((tm, w), lambda i: (i, 0))
    grp = pl.BlockSpec((NSA_KV, tm, LANES), lambda i: (0, i, 0))
    return pl.pallas_call(
        _nsa_prep_body,
        grid=(N // tm,),
        in_specs=[pl.BlockSpec((tm, QW), lambda i: (i, 2 * LRU_WIDTH // QW)),
                  pl.BlockSpec((tm, 6 * KW), lambda i: (i, (2 * LRU_WIDTH + QW) // (6 * KW))),
                  pl.BlockSpec((tm, LANES), lambda i: (i, (2 * LRU_WIDTH + QW + 6 * KW) // LANES)),
                  row(LANES), row(LANES),
                  pl.BlockSpec((1, QW), lambda i: (0, 0)),
                  pl.BlockSpec((3, LANES), lambda i: (0, 0)),
                  pl.BlockSpec((QW, QW), lambda i: (0, 0))],
        out_specs=[row(QW), row(QW), row(LANES), row(LANES), row(LANES), grp, grp, grp, grp, grp],
        out_shape=[f((N, QW)), f((N, QW)), f((N, LANES)), f((N, LANES)), f((N, LANES)),
                   b((NSA_KV, N, LANES)), b((NSA_KV, N, LANES)), b((NSA_KV, N, LANES)), b((NSA_KV, N, LANES)),
                   f((NSA_KV, N, LANES))],
        compiler_params=_cparams("parallel"),
        name="nsa_prep",
    )(z, z, z, cos, sin, jnp.tile(q_gain, NSA_HEADS)[None], jnp.tile(k_gain, (1, NSA_KV)), _head_mean_matrix(QW))


def compress_blocks(rows, w, B, T):
    nsb = T // SEL_LEN
    wc = jnp.einsum('jde,gh->jgdhe', w, jnp.eye(NSA_KV, dtype=w.dtype)).reshape(CMP_LEN * LANES, LANES)
    kc = matmul(rows.reshape(B * T // CMP_LEN, CMP_LEN * LANES), wc, tm=256)
    kc = kc.reshape(B, nsb, CMP_PER_SEL, NSA_KV, HEAD_DIM).transpose(0, 3, 2, 1, 4)
    kc = jnp.pad(kc, ((0, 0), (0, 0), (0, 0), (0, NSA_MAXBLK - nsb), (0, 0)))
    kc = kc.reshape(B, NSA_KV, CMP_PER_SEL * NSA_MAXBLK, HEAD_DIM)
    return jnp.concatenate([kc, kc], axis=-1)


def _flash_step(s, v, m_, l_, acc):
    m_new = jnp.maximum(m_, jnp.max(s, axis=-1, keepdims=True))
    alpha = jnp.exp(m_ - m_new)
    p = jnp.exp(s - m_new)
    l_new = alpha * l_ + jnp.sum(p, axis=-1, keepdims=True)
    return m_new, l_new, alpha * acc + jnp.dot(p.astype(BF16), v, preferred_element_type=F32)


def _nsa_body(qn_ref, qr_ref, gate_ref, kc_ref, vc_ref, sk_ref, sv_ref, wk_ref, wv_ref, e_ref, o_ref):
    i = pl.program_id(2)
    tq = NSA_TQ
    R = NSA_GROUP * tq
    q0 = i * tq
    scale = HEAD_DIM ** -0.5
    dn = (((1,), (1,)), ((), ()))
    lane = lax.broadcasted_iota(jnp.int32, (tq, LANES), 1)
    low = lane < HEAD_DIM
    qpos1 = q0 + lax.broadcasted_iota(jnp.int32, (tq, 1), 0)
    qpos = q0 + (lax.broadcasted_iota(jnp.int32, (R, 1), 0) & (tq - 1))
    unstack = lambda x: [x[m * tq:(m + 1) * tq] for m in range(NSA_GROUP)]

    def stack(ref):
        parts = []
        for m in range(NSA_GROUP):
            pair, half = divmod(m, 2)
            x = ref[:, pair * LANES:(pair + 1) * LANES] * scale
            parts.append(jnp.where(low if half == 0 else jnp.logical_not(low), x, 0.0))
        return jnp.concatenate(parts, axis=0).astype(BF16)

    cidx = lax.broadcasted_iota(jnp.int32, (R, CMP_PER_SEL * NSA_MAXBLK), 1)
    par = (cidx >= NSA_MAXBLK).astype(jnp.int32)
    blk = CMP_PER_SEL * (cidx - NSA_MAXBLK * par) + par
    cmask = (blk + 1) * CMP_LEN - 1 <= qpos
    s = jnp.where(cmask, lax.dot_general(stack(qn_ref), kc_ref[...].astype(BF16), dn, preferred_element_type=F32), NEG)
    mx = jnp.max(s, axis=-1, keepdims=True)
    e = jnp.where(cmask, jnp.exp(s - mx), 0.0)
    p = e / jnp.maximum(jnp.sum(e, axis=-1, keepdims=True), TINY)
    o_c = jnp.dot(p.astype(BF16), vc_ref[...].astype(BF16), preferred_element_type=F32)
    heads = unstack(p)
    psum = heads[0]
    for x in heads[1:]:
        psum = psum + x
    imp = psum[:, :NSA_MAXBLK] + psum[:, NSA_MAXBLK:]

    cur = lax.shift_right_logical(qpos1, SEL_LEN.bit_length() - 1)
    valid = lane * SEL_LEN <= qpos1
    forced = (lane == 0) | (lane == cur) | (lane == cur - 1)
    score = jnp.where(valid, imp + jnp.where(forced, FORCE, 0.0), NEG)
    lane_f = lane.astype(F32)

    def pick(_, carry):
        sc, sel = carry
        mx = jnp.max(sc, axis=-1, keepdims=True)
        first = jnp.min(jnp.where(sc == mx, lane_f, float(LANES)), axis=-1, keepdims=True)
        hit = lane_f == first
        sel = jnp.where(jnp.logical_and(hit, mx > 0.5 * NEG), 0.0, sel)
        return jnp.where(hit, M_INIT, sc), sel

    _, selneg = lax.fori_loop(0, N_SEL, pick, (score, jnp.full((tq, LANES), SEL_OFF, F32)), unroll=True)
    qr = stack(qr_ref)
    q_aug = jnp.concatenate([qr, jnp.concatenate([selneg.astype(BF16)] * NSA_GROUP, axis=0)], axis=1)

    def sweep(t, carry):
        ks = pl.multiple_of(t * NSA_TK, NSA_TK)
        k_aug = jnp.concatenate([sk_ref[pl.ds(ks, NSA_TK), :], e_ref[pl.ds(ks, NSA_TK), :]], axis=1)
        s = lax.dot_general(q_aug, k_aug, dn, preferred_element_type=F32)
        return _flash_step(s, sv_ref[pl.ds(ks, NSA_TK), :], *carry)

    n_full = q0 // NSA_TK
    init = (jnp.full((R, 1), M_INIT, F32), jnp.zeros((R, 1), F32), jnp.zeros((R, LANES), F32))
    carry = lax.fori_loop(0, n_full, sweep, init)
    ws = pl.multiple_of(jnp.maximum(q0 - SW_WIN, 0), NSA_TQ)
    kpos = ws + lax.broadcasted_iota(jnp.int32, (1, NSA_SPAN), 1)
    dist = qpos - kpos
    k_aug = jnp.concatenate([sk_ref[pl.ds(ws, NSA_SPAN), :], e_ref[pl.ds(ws, NSA_SPAN), :]], axis=1)
    s = jnp.where(jnp.logical_and(kpos >= n_full * NSA_TK, dist >= 0),
                  lax.dot_general(q_aug, k_aug, dn, preferred_element_type=F32), NEG)
    _, l_s, acc_s = _flash_step(s, sv_ref[pl.ds(ws, NSA_SPAN), :], *carry)
    o_s = acc_s / l_s

    s = jnp.where(jnp.logical_and(dist >= 0, dist <= SW_WIN),
                  lax.dot_general(qr, wk_ref[pl.ds(ws, NSA_SPAN), :], dn, preferred_element_type=F32), NEG)
    mx = jnp.max(s, axis=-1, keepdims=True)
    e = jnp.exp(s - mx)
    den = jnp.maximum(jnp.sum(e, axis=-1, keepdims=True), TINY)
    o_w = jnp.dot(e.astype(BF16), wv_ref[pl.ds(ws, NSA_SPAN), :], preferred_element_type=F32) / den

    gates = gate_ref[...]
    gate = lambda j: jnp.concatenate([gates[:, 3 * m + j:3 * m + j + 1] for m in range(NSA_GROUP)], axis=0)
    outs = unstack(gate(0) * o_c + gate(1) * o_s + gate(2) * o_w)
    for pair in range(NSA_GROUP // 2):
        o_ref[:, pair * LANES:(pair + 1) * LANES] = jnp.where(low, outs[2 * pair], outs[2 * pair + 1])


def nsa_attention(qn, qr, gates, kc, vc, skd, svd, wkd, wvd, B, T):
    assert T % NSA_TK == 0 and NSA_SPAN <= T <= NSA_MAXBLK * SEL_LEN and T // SEL_LEN >= N_SEL
    GW = NSA_GROUP * HEAD_DIM
    onehot = (jnp.arange(T)[:, None] // SEL_LEN == jnp.arange(NSA_MAXBLK)[None, :]).astype(BF16)
    qspec = pl.BlockSpec((None, NSA_TQ, GW), lambda b, g, i: (b, i, g))
    cspec = pl.BlockSpec((None, None, CMP_PER_SEL * NSA_MAXBLK, LANES), lambda b, g, i: (b, g, 0, 0))
    kspec = pl.BlockSpec((None, None, T, LANES), lambda b, g, i: (g, b, 0, 0))
    r4 = lambda a: a.reshape(NSA_KV, B, T, LANES)
    out = pl.pallas_call(
        _nsa_body,
        grid=(B, NSA_KV, T // NSA_TQ),
        in_specs=[qspec, qspec,
                  pl.BlockSpec((None, None, NSA_TQ, LANES), lambda b, g, i: (g, b, i, 0)),
                  cspec, cspec, kspec, kspec, kspec, kspec,
                  pl.BlockSpec((T, NSA_MAXBLK), lambda b, g, i: (0, 0))],
        out_specs=qspec,
        out_shape=jax.ShapeDtypeStruct((B, T, NSA_HEADS * HEAD_DIM), F32),
        compiler_params=_cparams("parallel", "parallel", "arbitrary"),
        name="nsa_attention",
    )(qn.reshape(B, T, -1), qr.reshape(B, T, -1), r4(gates), kc, vc, r4(skd), r4(svd), r4(wkd), r4(wvd), onehot)
    return out.reshape(B * T, NSA_HEADS * HEAD_DIM)


LRU_TM = 256
SUBLANES = 8


def _lru_body(u_ref, y_ref, cw_ref, cb_ref, wa_ref, ba_ref, wi_ref, bi_ref, sp_ref, h0_ref, cbuf_ref,
              o_ref, hlast_ref, ctail_ref, h_sc, tail_sc, a_sc, b_sc):
    tm = u_ref.shape[0]

    @pl.when(pl.program_id(1) == 0)
    def _():
        h_sc[...] = h0_ref[...]
        tail_sc[...] = cbuf_ref[...]

    u = u_ref[...]
    ext = jnp.concatenate([tail_sc[...], u], axis=0)
    uc = cb_ref[...]
    for j in range(CONV_W):
        lo = SUBLANES - (CONV_W - 1) + j
        uc = uc + ext[lo:lo + tm] * cw_ref[j:j + 1, :]
    tail_sc[...] = u[tm - SUBLANES:tm]
    ctail_ref[...] = u[tm - SUBLANES:tm]
    ub = uc.astype(BF16)
    r = jax.nn.sigmoid(jnp.dot(ub, wa_ref[...].astype(BF16), preferred_element_type=F32) + ba_ref[...])
    gi = jax.nn.sigmoid(jnp.dot(ub, wi_ref[...].astype(BF16), preferred_element_type=F32) + bi_ref[...])
    log_a = -LRU_C * r * sp_ref[...]
    a = jnp.exp(log_a)
    a_sc[...] = a
    b_sc[...] = jnp.sqrt(-jnp.tanh(log_a) * (a * a + 1.0)) * (gi * uc)

    def step(k, h):
        r0 = pl.multiple_of(k * SUBLANES, SUBLANES)
        a8 = a_sc[pl.ds(r0, SUBLANES), :]
        b8 = b_sc[pl.ds(r0, SUBLANES), :]
        rows = []
        for j in range(SUBLANES):
            h = a8[j:j + 1] * h + b8[j:j + 1]
            rows.append(h)
        b_sc[pl.ds(r0, SUBLANES), :] = jnp.concatenate(rows, axis=0)
        return h

    h = lax.fori_loop(0, tm // SUBLANES, step, h_sc[...])
    h_sc[...] = h
    hlast_ref[...] = h
    o_ref[...] = b_sc[...] * jax.nn.gelu(y_ref[...])


def rg_lru_prompt(z, B, T, h0, conv_buf, P, e):
    W = LRU_WIDTH
    tm = LRU_TM
    assert T % tm == 0 and tm >= SUBLANES and CONV_W - 1 <= SUBLANES
    nt = T // tm
    bd = lambda w: jnp.einsum('kij,kl->kilj', w, jnp.eye(LRU_BLOCKS, dtype=w.dtype)).reshape(W, W)
    vec = lambda v: v.reshape(1, W)
    cbuf = jnp.pad(conv_buf, ((0, 0), (SUBLANES - (CONV_W - 1), 0), (0, 0)))
    const = lambda shape: pl.BlockSpec(shape, lambda b, i: (0,) * len(shape))
    per_b = lambda rows: pl.BlockSpec((None, rows, W), lambda b, i: (b, 0, 0))
    out, h_last, ctail = pl.pallas_call(
        _lru_body,
        grid=(B, nt),
        in_specs=[pl.BlockSpec((tm, W), lambda b, i: (b * nt + i, 0)),
                  pl.BlockSpec((tm, W), lambda b, i: (b * nt + i, 1)),
                  const((CONV_W, W)), const((1, W)), const((W, W)), const((1, W)), const((W, W)), const((1, W)),
                  const((1, W)), per_b(1), per_b(SUBLANES)],
        out_specs=[pl.BlockSpec((tm, W), lambda b, i: (b * nt + i, 0)), per_b(1), per_b(SUBLANES)],
        out_shape=[jax.ShapeDtypeStruct((B * T, W), F32), jax.ShapeDtypeStruct((B, 1, W), F32),
                   jax.ShapeDtypeStruct((B, SUBLANES, W), F32)],
        scratch_shapes=[pltpu.VMEM((1, W), F32), pltpu.VMEM((SUBLANES, W), F32),
                        pltpu.VMEM((tm, W), F32), pltpu.VMEM((tm, W), F32)],
        compiler_params=_cparams("parallel", "arbitrary"),
        name="rg_lru",
    )(z, z, P['conv_w'][e], vec(P['conv_b'][e]), bd(P['lru_wa'][e]), vec(P['lru_ba'][e]), bd(P['lru_wi'][e]),
      vec(P['lru_bi'][e]), vec(jax.nn.softplus(-P['lru_lambda'][e].astype(F32))), h0[:, None, :], cbuf)
    return out, h_last[:, 0], ctail[:, SUBLANES - (CONV_W - 1):]


def even_mixer_prompt(h, P, e):
    B, T, D = h.shape
    N = B * T
    w_in = jnp.pad(P['w_in_even'][e], ((0, 0), (0, EVEN_PAD - EVEN_COLS)))
    z = matmul(h.reshape(N, D), w_in)
    lru_out, h_last, conv_new = rg_lru_prompt(z, B, T, jnp.zeros((B, LRU_WIDTH), F32),
                                              jnp.zeros((B, CONV_W - 1, LRU_WIDTH), F32), P, e)
    cos, sin = _rope_tables(jnp.tile(jnp.arange(T), B))
    qn, qr, ck, sk, wk, skd, svd, wkd, wvd, gates = nsa_prep(z, cos, sin, P['nsa_q_gain'][e], P['nsa_k_gain'][e])
    base = 2 * LRU_WIDTH + NSA_HEADS * HEAD_DIM
    kvcol = lambda c: z[:, base + c * LANES:base + (c + 1) * LANES]
    cv, sv, wv = kvcol(1), kvcol(3), kvcol(5)
    kc = compress_blocks(ck, P['w_cmp_k'][e], B, T)
    vc = compress_blocks(cv, P['w_cmp_v'][e], B, T)
    nsa_out = nsa_attention(qn, qr, gates, kc, vc, skd, svd, wkd, wvd, B, T)
    mix = matmul(jnp.concatenate([lru_out, nsa_out], axis=-1), P['w_out_even'][e])
    kvs = (B, T, NSA_KV, HEAD_DIM)
    keep = min(SW_WIN, T)
    r = lambda a: a.reshape(kvs)
    return mix.reshape(B, T, D), (h_last, conv_new, r(ck), r(cv), r(sk), r(sv), r(wk)[:, -keep:], r(wv)[:, -keep:])


def _bf16_round(x):
    return x.astype(BF16).astype(F32)


def _split_dot(x, m):
    hi = x.astype(BF16)
    lo = (x - hi.astype(F32)).astype(BF16)
    return jnp.dot(hi, m, preferred_element_type=F32) + jnp.dot(lo, m, preferred_element_type=F32)


def _dil_decode_body(q_ref, kn_ref, vn_ref, k1_ref, k4_ref, k16_ref, v1_ref, v4_ref, v16_ref, hsum_ref, hexp_ref, o_ref):
    nq = o_ref.shape[0]
    W = C_WIDTH
    scale = HEAD_DIM ** -0.5
    hsum = hsum_ref[...]
    hexp = hexp_ref[...]
    crow = lax.broadcasted_iota(jnp.int32, (DIL_TQ, 1), 0)
    nrow = lax.broadcasted_iota(jnp.int32, (q_ref.shape[0], 1), 0)
    kn = _bf16_round(kn_ref[...])
    vn = _bf16_round(vn_ref[...])
    k1 = _bf16_round(k1_ref[...])
    v1 = _bf16_round(v1_ref[...])
    outs = []
    for t in range(nq):
        qt = _bf16_round(q_ref[t:t + 1, :] * scale)
        cs = slice(t * W, (t + 1) * W)
        keys = (k1, _bf16_round(k4_ref[:, cs]), _bf16_round(k16_ref[:, cs]))
        vals = (v1, _bf16_round(v4_ref[:, cs]), _bf16_round(v16_ref[:, cs]))
        s = [_split_dot(k * qt, hsum) for k in keys]
        s[0] = jnp.where(crow >= t, s[0], NEG)
        s_new = _split_dot(kn * qt, hsum)
        mult = (nrow <= t).astype(F32) + (len(DIL_PATTERNS) - 1) * (nrow == t).astype(F32)
        s_new = jnp.where(nrow <= t, s_new, NEG)
        m = jnp.max(s_new, axis=0, keepdims=True)
        for x in s:
            m = jnp.maximum(m, jnp.max(x, axis=0, keepdims=True))
        e = [jnp.exp(x - m) for x in s]
        e_new = mult * jnp.exp(s_new - m)
        den = jnp.sum(e_new, axis=0, keepdims=True)
        for x in e:
            den = den + jnp.sum(x, axis=0, keepdims=True)
        den = jnp.maximum(den, TINY)
        acc = jnp.sum(jnp.dot((e_new / den).astype(BF16), hexp, preferred_element_type=F32) * vn, axis=0, keepdims=True)
        for x, v in zip(e, vals):
            p = jnp.dot((x / den).astype(BF16), hexp, preferred_element_type=F32)
            acc = acc + jnp.sum(p * v, axis=0, keepdims=True)
        outs.append(acc)
    o_ref[...] = jnp.concatenate(outs, axis=0)


def dilated_decode(z, cache_k, cache_v):
    B, Tc = cache_k.shape[:2]
    Q = z.shape[0] // B
    W = C_WIDTH
    (w1, d1), (w4, d4), (w16, d16) = DIL_PATTERNS
    assert d1 == 1 and all(w // d == DIL_TQ and Tc % (d * DIL_TQ) == 0 and (d == 1 or Q <= d) for w, d in DIL_PATTERNS)
    assert Q <= DIL_TQ and len(DIL_PATTERNS) == 3
    NR = 16
    zv = jnp.pad(z.reshape(B, Q, 3 * W), ((0, 0), (0, NR - Q), (0, 0)))
    new = lambda c: pl.BlockSpec((None, NR, W), lambda b: (b, 0, c))
    view = lambda a, d: a.reshape(B, Tc // d, d * W)
    last = lambda d: pl.BlockSpec((None, DIL_TQ, min(d, Q) * W), lambda b: (b, Tc // d // DIL_TQ - 1, 0))
    head = jnp.arange(W)[:, None] // HEAD_DIM == jnp.arange(LANES)[None, :]
    out = pl.pallas_call(
        _dil_decode_body,
        grid=(B,),
        in_specs=[new(0), new(1), new(2), last(d1), last(d4), last(d16), last(d1), last(d4), last(d16),
                  pl.BlockSpec((W, LANES), lambda b: (0, 0)), pl.BlockSpec((LANES, W), lambda b: (0, 0))],
        out_specs=pl.BlockSpec((None, Q, W), lambda b: (b, 0, 0)),
        out_shape=jax.ShapeDtypeStruct((B, Q, W), F32),
        compiler_params=_cparams("parallel"),
        name="dilated_decode",
    )(zv, zv, zv, view(cache_k, d1), view(cache_k, d4), view(cache_k, d16),
      view(cache_v, d1), view(cache_v, d4), view(cache_v, d16), head.astype(BF16), head.T.astype(BF16))
    return out.reshape(B * Q, W)


DEC_FC = 256
DEC_NR = 8


def _dil_window_body(q_ref, kt_ref, vt_ref, knt_ref, vnt_ref, ko_ref, vo_ref, att_ref, *, nq):
    FC, Tc = kt_ref.shape
    NR = q_ref.shape[0]
    nh = FC // HEAD_DIM
    kt, vt, knt, vnt = kt_ref[...], vt_ref[...], knt_ref[...], vnt_ref[...]
    keep = lax.broadcasted_iota(jnp.int32, (FC, LANES), 1) < LANES - nq
    for src, new, out in ((kt, knt, ko_ref), (vt, vnt, vo_ref)):
        r = pltpu.roll(src, Tc - nq, 1)
        out[:, :Tc - LANES] = r[:, :Tc - LANES]
        out[:, Tc - LANES:] = jnp.where(keep, r[:, Tc - LANES:], new)

    q = q_ref[...] * (HEAD_DIM ** -0.5)
    fhead = lax.broadcasted_iota(jnp.int32, (NR, FC), 1) // HEAD_DIM
    qbd = jnp.concatenate([jnp.where(fhead == h, q, 0.0) for h in range(nh)], axis=0).astype(BF16)
    s = jnp.dot(qbd, jnp.concatenate([kt, knt], axis=1).astype(BF16), preferred_element_type=F32)
    R, C = s.shape
    col = lax.broadcasted_iota(jnp.int32, (R, C), 1)
    pos = jnp.where(col < Tc, col, col - (LANES - nq))
    real = jnp.logical_or(col < Tc, col >= Tc + LANES - nq)
    dist = Tc + (lax.broadcasted_iota(jnp.int32, (R, 1), 0) & (NR - 1)) - pos
    cnt = jnp.zeros((R, C), F32)
    for w, d in DIL_PATTERNS:
        hit = jnp.logical_and(jnp.logical_and(dist >= 0, dist <= w), (dist & (d - 1)) == 0)
        cnt = cnt + jnp.where(jnp.logical_and(hit, real), 1.0, 0.0)
    s = jnp.where(cnt > 0.0, s, NEG)
    e = cnt * jnp.exp(s - jnp.max(s, axis=-1, keepdims=True))
    p = e / jnp.maximum(jnp.sum(e, axis=-1, keepdims=True), TINY)
    o = lax.dot_general(p.astype(BF16), jnp.concatenate([vt, vnt], axis=1).astype(BF16), (((1,), (1,)), ((), ())),
                        preferred_element_type=F32)
    att = jnp.where(fhead == 0, o[0:NR], 0.0)
    for h in range(1, nh):
        att = att + jnp.where(fhead == h, o[h * NR:(h + 1) * NR], 0.0)
    att_ref[...] = att


def dilated_window_step(z, cache_k, cache_v):
    B, Tc = cache_k.shape[:2]
    Q = z.shape[0] // B
    W = C_WIDTH
    assert all(d & (d - 1) == 0 and w <= Tc for w, d in DIL_PATTERNS) and Q <= DEC_NR and Tc % LANES == 0
    assert W % DEC_FC == 0 and DEC_FC % HEAD_DIM == 0
    to_ft = lambda c: c.transpose(0, 2, 3, 1).reshape(B, W, Tc)
    from_ft = lambda c: c.reshape(B, C_HEADS, HEAD_DIM, Tc).transpose(0, 3, 1, 2)
    z4 = z.reshape(B, Q, 3, W)
    q = jnp.pad(z4[:, :, 0], ((0, 0), (0, DEC_NR - Q), (0, 0)))
    new_t = lambda c: jnp.pad(z4[:, :, c].transpose(0, 2, 1), ((0, 0), (0, 0), (LANES - Q, 0)))
    feat = lambda lanes: pl.BlockSpec((None, DEC_FC, lanes), lambda b, f: (b, f, 0))
    qspec = pl.BlockSpec((None, DEC_NR, DEC_FC), lambda b, f: (b, 0, f))
    ko, vo, att = pl.pallas_call(
        functools.partial(_dil_window_body, nq=Q),
        grid=(B, W // DEC_FC),
        in_specs=[qspec, feat(Tc), feat(Tc), feat(LANES), feat(LANES)],
        out_specs=[feat(Tc), feat(Tc), qspec],
        out_shape=[jax.ShapeDtypeStruct((B, W, Tc), F32), jax.ShapeDtypeStruct((B, W, Tc), F32),
                   jax.ShapeDtypeStruct((B, DEC_NR, W), F32)],
        compiler_params=_cparams("parallel", "parallel"),
        name="dilated_window_step",
    )(q, to_ft(cache_k), to_ft(cache_v), new_t(1), new_t(2))
    return att[:, :Q].reshape(B * Q, W), from_ft(ko), from_ft(vo)


def odd_mixer_sample(h, p0, past, P, o):
    buf_k, buf_v = past
    B, T, D = h.shape
    assert buf_k.shape[1] == C_WINDOW and p0 >= C_WINDOW
    cos, sin = _rope_tables(jnp.tile(p0 + jnp.arange(T), B))
    z = inproj_odd(h.reshape(B * T, D), P['w_in_odd'][o], P['dil_q_gain'][o], P['dil_k_gain'][o], cos, sin)
    att, k_all, v_all = dilated_window_step(z, buf_k, buf_v)
    mix = matmul(att, P['w_out_odd'][o])
    return mix.reshape(B, T, D), (k_all, v_all)


def rms_norm(x, g):
    xf = x.astype(jnp.float32)
    y = xf * lax.rsqrt(jnp.mean(xf * xf, axis=-1, keepdims=True) + EPS)
    return (y * g.astype(jnp.float32)).astype(x.dtype)


def rope(x, pos):
    half = HEAD_DIM // 2
    freq = ROPE_THETA ** (-jnp.arange(half, dtype=jnp.float32) / half)
    ang = pos.astype(jnp.float32)[:, None] * freq[None, :]
    cos = jnp.cos(ang)[None, :, None, :]
    sin = jnp.sin(ang)[None, :, None, :]
    xf = x.astype(jnp.float32)
    x1, x2 = xf[..., :half], xf[..., half:]
    return jnp.concatenate([x1 * cos - x2 * sin, x2 * cos + x1 * sin], axis=-1).astype(x.dtype)


def masked_softmax(s, mask):
    s = jnp.where(mask, s.astype(jnp.float32), NEG)
    m = jnp.max(s, axis=-1, keepdims=True)
    e = jnp.where(mask, jnp.exp(s - m), 0.0)
    den = jnp.maximum(jnp.sum(e, axis=-1, keepdims=True), TINY)
    return e / den, m + jnp.log(den)


def split_cols(z, sizes):
    out, o = [], 0
    for s in sizes:
        out.append(z[..., o:o + s])
        o += s
    return out


def over_query_blocks(fn, T):
    if T > Q_BLOCK and T % Q_BLOCK == 0:
        out = lax.map(lambda i: fn(i * Q_BLOCK, Q_BLOCK), jnp.arange(T // Q_BLOCK))
        return jnp.swapaxes(out, 0, 1).reshape(out.shape[1], T, out.shape[-1])
    return fn(0, T)


def gather_pages(pool, page_table):
    g = pool[page_table]
    return g.reshape(g.shape[0], g.shape[1] * g.shape[2], g.shape[3], g.shape[4])


def causal_conv(u, buf, w, b):
    T = u.shape[1]
    up = jnp.concatenate([buf, u], axis=1)
    y = b + up[:, 0:T] * w[0]
    for j in range(1, CONV_W):
        y = y + up[:, j:j + T] * w[j]
    return y, up[:, T:]


def _lin_comb(l, r):
    return (l[0] * r[0], r[0] * l[1] + r[1])


def rg_lru(u, h0, wa, ba, wi, bi, lam):
    B, T, W = u.shape
    ub = u.reshape(B, T, LRU_BLOCKS, LRU_BW)
    r = jax.nn.sigmoid((jnp.einsum('btki,kij->btkj', ub, wa).reshape(B, T, W) + ba).astype(jnp.float32))
    gi = jax.nn.sigmoid((jnp.einsum('btki,kij->btkj', ub, wi).reshape(B, T, W) + bi).astype(jnp.float32))
    log_a = -LRU_C * r * jax.nn.softplus(-lam.astype(jnp.float32))
    a = jnp.exp(log_a)
    b = jnp.sqrt(-jnp.expm1(2.0 * log_a)) * (gi * u.astype(jnp.float32))
    b = b.at[:, 0].add(a[:, 0] * h0.astype(jnp.float32))
    _, h = lax.associative_scan(_lin_comb, (a, b), axis=1)
    return h.astype(u.dtype), h[:, -1].astype(u.dtype)


def compress(rows, w, t_pad):
    B, T = rows.shape[:2]
    r = jnp.pad(rows, ((0, 0), (0, t_pad - T), (0, 0), (0, 0)))
    r = r.reshape(B, t_pad // CMP_LEN, CMP_LEN, NSA_KV, HEAD_DIM)
    return jnp.einsum('bnjgd,jde->bnge', r, w)


def sel_blocks(rows, t_pad):
    B, T = rows.shape[:2]
    r = jnp.pad(rows, ((0, 0), (0, t_pad - T), (0, 0), (0, 0)))
    return r.reshape(B, t_pad // SEL_LEN, SEL_LEN, NSA_KV, HEAD_DIM).transpose(0, 3, 1, 2, 4)


def nsa_attend(q, q_rot, gates, qpos, wstart, kc, vc, ksg, vsg, kw_p, vw_p):
    B, Q = q.shape[:2]
    G, M = NSA_KV, NSA_GROUP
    scale = HEAD_DIM ** -0.5
    qg = q.reshape(B, Q, G, M, HEAD_DIM)
    qr = q_rot.reshape(B, Q, G, M, HEAD_DIM)
    ncb = kc.shape[1]
    s_c = jnp.einsum('bqgmd,bngd->bgmqn', qg, kc) * scale
    cmask = ((jnp.arange(ncb) + 1) * CMP_LEN - 1)[None, :] <= qpos[:, None]
    p_c, _ = masked_softmax(s_c, cmask)
    o_c = jnp.einsum('bgmqn,bngd->bqgmd', p_c.astype(vc.dtype), vc)
    nsb = ksg.shape[2]
    imp = p_c.sum(axis=2).reshape(B, G, Q, nsb, CMP_PER_SEL).sum(-1)
    sb = jnp.arange(nsb)[None, :]
    cur = (qpos // SEL_LEN)[:, None]
    valid = sb * SEL_LEN <= qpos[:, None]
    forced = (sb == 0) | (sb == cur) | (sb == cur - 1)
    score = jnp.where(valid, imp + jnp.where(forced, FORCE, 0.0), NEG)
    top_s, top_i = lax.top_k(score, min(N_SEL, nsb))
    n = top_i.shape[-1]
    bi = jnp.arange(B)[:, None, None, None]
    gi = jnp.arange(G)[None, :, None, None]
    k_sel = ksg[bi, gi, top_i]
    v_sel = vsg[bi, gi, top_i]
    kpos = top_i[..., None] * SEL_LEN + jnp.arange(SEL_LEN)
    smask = (top_s > 0.5 * NEG)[..., None] & (kpos <= qpos[None, None, :, None, None])
    s_s = jnp.einsum('bqgmd,bgqnld->bgmqnl', qr, k_sel).reshape(B, G, M, Q, n * SEL_LEN) * scale
    p_s, _ = masked_softmax(s_s, smask.reshape(B, G, 1, Q, n * SEL_LEN))
    o_s = jnp.einsum('bgmqk,bgqkd->bqgmd', p_s.astype(v_sel.dtype), v_sel.reshape(B, G, Q, n * SEL_LEN, HEAD_DIM))
    span = SW_WIN + Q
    kwb = lax.dynamic_slice_in_dim(kw_p, wstart, span, axis=1)
    vwb = lax.dynamic_slice_in_dim(vw_p, wstart, span, axis=1)
    qloc = wstart + jnp.arange(Q)
    kloc = wstart - SW_WIN + jnp.arange(span)
    dist = qloc[:, None] - kloc[None, :]
    wmask = (dist >= 0) & (dist <= SW_WIN) & (kloc[None, :] >= 0)
    s_w = jnp.einsum('bqgmd,bkgd->bgmqk', qr, kwb) * scale
    p_w, _ = masked_softmax(s_w, wmask)
    o_w = jnp.einsum('bgmqk,bkgd->bqgmd', p_w.astype(vwb.dtype), vwb)
    gt = gates.reshape(B, Q, G, M, 3).astype(o_c.dtype)
    o = gt[..., 0:1] * o_c + gt[..., 1:2] * o_s + gt[..., 2:3] * o_w
    return o.reshape(B, Q, NSA_HEADS * HEAD_DIM)


def even_mixer(h, p0, past, P, e):
    lru_h0, conv_buf, past_ck, past_cv, past_sk, past_sv, win_k, win_v = past
    B, T, _ = h.shape
    z = matmul(h, P['w_in_even'][e])
    u, y, q, ck, cv, sk, sv, wk, wv, g = split_cols(z, EVEN_SIZES)
    uc, conv_new = causal_conv(u, conv_buf, P['conv_w'][e], P['conv_b'][e])
    hl, h_last = rg_lru(uc, lru_h0, P['lru_wa'][e], P['lru_ba'][e], P['lru_wi'][e], P['lru_bi'][e], P['lru_lambda'][e])
    lru_out = hl * jax.nn.gelu(y)
    pos = p0 + jnp.arange(T)
    kg = P['nsa_k_gain'][e]
    kvs = (B, T, NSA_KV, HEAD_DIM)
    q = rms_norm(q.reshape(B, T, NSA_HEADS, HEAD_DIM), P['nsa_q_gain'][e])
    q_rot = rope(q, pos)
    ck = rms_norm(ck.reshape(kvs), kg[0])
    cv = cv.reshape(kvs)
    sk = rope(rms_norm(sk.reshape(kvs), kg[1]), pos)
    sv = sv.reshape(kvs)
    wk = rope(rms_norm(wk.reshape(kvs), kg[2]), pos)
    wv = wv.reshape(kvs)
    gates = jax.nn.sigmoid(g.astype(jnp.float32)).reshape(B, T, NSA_HEADS, 3)
    ck_all = jnp.concatenate([past_ck, ck], axis=1)
    cv_all = jnp.concatenate([past_cv, cv], axis=1)
    sk_all = jnp.concatenate([past_sk, sk], axis=1)
    sv_all = jnp.concatenate([past_sv, sv], axis=1)
    t_k = ck_all.shape[1]
    t_pad = -(-t_k // SEL_LEN) * SEL_LEN
    kc = compress(ck_all, P['w_cmp_k'][e], t_pad)
    vc = compress(cv_all, P['w_cmp_v'][e], t_pad)
    ksg = sel_blocks(sk_all, t_pad)
    vsg = sel_blocks(sv_all, t_pad)
    wk_all = jnp.concatenate([win_k, wk], axis=1)
    wv_all = jnp.concatenate([win_v, wv], axis=1)
    wb = win_k.shape[1]
    pad_w = ((0, 0), (SW_WIN, 0), (0, 0), (0, 0))
    wk_p = jnp.pad(wk_all, pad_w)
    wv_p = jnp.pad(wv_all, pad_w)

    def block(q0, qb):
        sl = lambda a: lax.dynamic_slice_in_dim(a, q0, qb, axis=1)
        qpos = p0 + q0 + jnp.arange(qb)
        return nsa_attend(sl(q), sl(q_rot), sl(gates), qpos, wb + q0, kc, vc, ksg, vsg, wk_p, wv_p)

    nsa_out = over_query_blocks(block, T)
    mix = matmul(jnp.concatenate([lru_out, nsa_out.astype(lru_out.dtype)], axis=-1), P['w_out_even'][e])
    keep = min(SW_WIN, wk_all.shape[1])
    return mix, (h_last, conv_new, ck, cv, sk, sv, wk_all[:, -keep:], wv_all[:, -keep:])


def dilated_attend(q, kp, vp, qloc):
    scale = HEAD_DIM ** -0.5
    outs, lses = [], []
    for w, d in DIL_PATTERNS:
        kl = qloc[:, None] - d * jnp.arange(w // d + 1)[None, :]
        kg = kp[:, kl + C_WINDOW]
        vg = vp[:, kl + C_WINDOW]
        s = jnp.einsum('bqhd,bqnhd->bhqn', q, kg) * scale
        p, lse = masked_softmax(s, kl >= 0)
        outs.append(jnp.einsum('bhqn,bqnhd->bqhd', p.astype(vg.dtype), vg))
        lses.append(lse)
    wts = jax.nn.softmax(jnp.stack(lses), axis=0)
    o = jnp.swapaxes(wts[0], 1, 2).astype(outs[0].dtype) * outs[0]
    for i in range(1, len(DIL_PATTERNS)):
        o = o + jnp.swapaxes(wts[i], 1, 2).astype(outs[i].dtype) * outs[i]
    return o.reshape(o.shape[0], o.shape[1], C_WIDTH)


def odd_mixer(h, p0, past, P, o):
    buf_k, buf_v = past
    B, T, _ = h.shape
    z = matmul(h, P['w_in_odd'][o])
    q, k, v = split_cols(z, (C_WIDTH,) * 3)
    shp = (B, T, C_HEADS, HEAD_DIM)
    pos = p0 + jnp.arange(T)
    q = rope(rms_norm(q.reshape(shp), P['dil_q_gain'][o]), pos)
    k = rope(rms_norm(k.reshape(shp), P['dil_k_gain'][o]), pos)
    v = v.reshape(shp)
    k_all = jnp.concatenate([buf_k, k], axis=1)
    v_all = jnp.concatenate([buf_v, v], axis=1)
    wb = buf_k.shape[1]
    pad = ((0, 0), (C_WINDOW, 0), (0, 0), (0, 0))
    kp = jnp.pad(k_all, pad)
    vp = jnp.pad(v_all, pad)

    def block(q0, qb):
        return dilated_attend(lax.dynamic_slice_in_dim(q, q0, qb, axis=1), kp, vp, wb + q0 + jnp.arange(qb))

    att = over_query_blocks(block, T)
    keep = min(C_WINDOW, k_all.shape[1])
    return matmul(att, P['w_out_odd'][o]), (k_all[:, -keep:], v_all[:, -keep:])


def _expert_ffn_body(blk_e_ref, n_used_ref, x_ref, wg_ref, wu_ref, wd_ref, o_ref):
    @pl.when(pl.program_id(0) < n_used_ref[0])
    def _():
        x = x_ref[...].astype(BF16)
        g = jnp.dot(x, wg_ref[...].astype(BF16), preferred_element_type=F32)
        u = jnp.dot(x, wu_ref[...].astype(BF16), preferred_element_type=F32)
        hid = (g * jax.nn.sigmoid(g)) * u
        o_ref[...] = jnp.dot(hid.astype(BF16), wd_ref[...].astype(BF16), preferred_element_type=F32)

    @pl.when(pl.program_id(0) >= n_used_ref[0])
    def _():
        o_ref[...] = jnp.zeros_like(o_ref)


def expert_ffn(xg, blk_e, n_used, w_gate, w_up, w_down, layer, rows):
    n_slots, D = xg.shape
    DE = w_gate.shape[-1]
    wspec = lambda a, b: pl.BlockSpec((None, None, a, b), lambda j, be, nu: (layer, be[j], 0, 0))
    return pl.pallas_call(
        _expert_ffn_body,
        grid_spec=pltpu.PrefetchScalarGridSpec(
            num_scalar_prefetch=2,
            grid=(n_slots // rows,),
            in_specs=[pl.BlockSpec((rows, D), lambda j, be, nu: (j, 0)), wspec(D, DE), wspec(D, DE), wspec(DE, D)],
            out_specs=pl.BlockSpec((rows, D), lambda j, be, nu: (j, 0))),
        out_shape=jax.ShapeDtypeStruct((n_slots, D), F32),
        compiler_params=_cparams("arbitrary"),
        name="expert_ffn",
    )(blk_e, n_used, xg, w_gate, w_up, w_down)


def routed_experts(xf, eid, wts, w_gate, w_up, w_down, layer):
    N, D = xf.shape
    M = N * TOP_K
    rows = 256 if M >= 256 * N_EXPERTS else MOE_BLK
    e_flat = eid.reshape(M)
    onehot = (e_flat[:, None] == jnp.arange(N_EXPERTS)[None, :]).astype(jnp.int32)
    seen = jnp.cumsum(onehot, axis=0)
    counts = seen[-1]
    rank = jnp.sum(seen * onehot, axis=1) - 1
    padded = (counts + rows - 1) // rows * rows
    pend = jnp.cumsum(padded)
    pstart = pend - padded
    dest = (pstart[e_flat] + rank).astype(jnp.int32)
    n_blocks = -(-(M + N_EXPERTS * (rows - 1)) // rows)
    n_slots = n_blocks * rows
    slot_tok = jnp.zeros((n_slots,), jnp.int32).at[dest].set(jnp.arange(M, dtype=jnp.int32) // TOP_K)
    blk_start = jnp.arange(n_blocks, dtype=jnp.int32) * rows
    blk_e = jnp.minimum(jnp.sum((pend[None, :] <= blk_start[:, None]).astype(jnp.int32), axis=1), N_EXPERTS - 1)
    n_used = (pend[-1:] // rows).astype(jnp.int32)
    yb = expert_ffn(xf[slot_tok], blk_e.astype(jnp.int32), n_used, w_gate, w_up, w_down, layer, rows)
    pos = dest.reshape(N, TOP_K)
    y = yb[pos[:, 0]] * wts[:, 0:1].astype(xf.dtype)
    for k in range(1, TOP_K):
        y = y + yb[pos[:, k]] * wts[:, k:k + 1].astype(xf.dtype)
    return y


def hier_moe(x, P, layer):
    B, T, D = x.shape
    N = B * T
    xf = x.reshape(N, D)
    rows = jnp.arange(N)
    w_router = jnp.concatenate([P['w_router_group'][layer], P['w_router_exp'][layer]], axis=1)
    logits = matmul(xf, jnp.pad(w_router, ((0, 0), (0, LANES - N_GROUPS - N_EXPERTS))))
    g_logit = (logits[:, :N_GROUPS] + P['b_router_group'][layer]).astype(jnp.float32)
    g_top = jnp.argmax(g_logit, axis=-1)
    g_w = jax.nn.softmax(g_logit, axis=-1)[rows, g_top]
    e_logit = (logits[:, N_GROUPS:N_GROUPS + N_EXPERTS] + P['b_router_exp'][layer]).astype(jnp.float32)
    e_logit = e_logit.reshape(N, N_GROUPS, EXP_PER_GROUP)[rows, g_top]
    top_l, top_i = lax.top_k(e_logit, TOP_K)
    top_p = jax.nn.softmax(top_l, axis=-1)
    eid = (g_top[:, None] * EXP_PER_GROUP + top_i).astype(jnp.int32)
    wts = g_w[:, None] * top_p
    y = routed_experts(xf, eid, wts, P['w_exp_gate'], P['w_exp_up'], P['w_exp_down'], layer)
    return y.reshape(B, T, D)


def trunk(x, p0, even_past, odd_past, P):
    even_new, odd_new = [], []
    for layer in range(DEPTH):
        hn = rms_norm(x, P['norm_mix'][layer])
        if layer % 2 == 0:
            if even_past is None:
                mix, st = even_mixer_prompt(hn, P, layer // 2)
            else:
                mix, st = even_mixer(hn, p0, even_past[layer // 2], P, layer // 2)
            even_new.append(st)
        else:
            if odd_past is None:
                mix, st = odd_mixer_prompt(hn, P, layer // 2)
            else:
                mix, st = odd_mixer_sample(hn, p0, odd_past[layer // 2], P, layer // 2)
            odd_new.append(st)
        x = x + mix.astype(x.dtype)
        x = x + hier_moe(rms_norm(x, P['norm_ffn'][layer]), P, layer).astype(x.dtype)
    return x, even_new, odd_new


def stack_layers(states):
    return [jnp.stack(items) for items in zip(*states)]


def kernel(x_prompt, x_sample, state_lru_h, state_lru_conv, cache_cmp_k, cache_cmp_v, cache_sel_k, cache_sel_v, cache_win_k, cache_win_v, cache_dil_k, cache_dil_v, page_table, norm_mix, norm_ffn, w_in_even, conv_w, conv_b, lru_wa, lru_ba, lru_wi, lru_bi, lru_lambda, nsa_q_gain, nsa_k_gain, w_cmp_k, w_cmp_v, w_out_even, w_in_odd, dil_q_gain, dil_k_gain, w_out_odd, w_router_group, b_router_group, w_router_exp, b_router_exp, w_exp_gate, w_exp_up, w_exp_down):
    P = dict(norm_mix=norm_mix, norm_ffn=norm_ffn, w_in_even=w_in_even, conv_w=conv_w, conv_b=conv_b,
             lru_wa=lru_wa, lru_ba=lru_ba, lru_wi=lru_wi, lru_bi=lru_bi, lru_lambda=lru_lambda,
             nsa_q_gain=nsa_q_gain, nsa_k_gain=nsa_k_gain, w_cmp_k=w_cmp_k, w_cmp_v=w_cmp_v,
             w_out_even=w_out_even, w_in_odd=w_in_odd, dil_q_gain=dil_q_gain, dil_k_gain=dil_k_gain,
             w_out_odd=w_out_odd, w_router_group=w_router_group, b_router_group=b_router_group,
             w_router_exp=w_router_exp, b_router_exp=b_router_exp, w_exp_gate=w_exp_gate,
             w_exp_up=w_exp_up, w_exp_down=w_exp_down)
    y_prompt, ev_p, od_p = trunk(x_prompt, 0, None, None, P)
    p0 = page_table.shape[1] * PAGE_SIZE
    ev1 = [(state_lru_h[e], state_lru_conv[e],
            gather_pages(cache_cmp_k[e], page_table), gather_pages(cache_cmp_v[e], page_table),
            gather_pages(cache_sel_k[e], page_table), gather_pages(cache_sel_v[e], page_table),
            cache_win_k[e], cache_win_v[e]) for e in range(N_EVEN)]
    od1 = [(cache_dil_k[o], cache_dil_v[o]) for o in range(N_ODD)]
    y_sample, ev_s, od_s = trunk(x_sample, p0, ev1, od1, P)
    p_lru_h, p_lru_conv, p_cmp_k, p_cmp_v, p_sel_k, p_sel_v, p_win_k, p_win_v = stack_layers(ev_p)
    p_dil_k, p_dil_v = stack_layers(od_p)
    s_lru_h, s_lru_conv, s_cmp_k, s_cmp_v, s_sel_k, s_sel_v, s_win_k, s_win_v = stack_layers(ev_s)
    s_dil_k, s_dil_v = stack_layers(od_s)
    return (y_prompt, y_sample, p_lru_h, p_lru_conv, p_cmp_k, p_cmp_v, p_sel_k, p_sel_v, p_win_k, p_win_v, p_dil_k, p_dil_v, s_lru_h, s_lru_conv, s_cmp_k, s_cmp_v, s_sel_k, s_sel_v, s_win_k, s_win_v, s_dil_k, s_dil_v)
```
